```python
import math
import jax, jax.numpy as jnp
from jax import lax
import numpy as np

D_MODEL = 4096
BATCH = 4
SEQ = 2048
DEPTH = 2
DEC_BATCH = 32
DEC_SEQ = 16
PAST_LEN = 4096

CHUNK = 64
N_MIXERS = 2
EPS = 1e-5
HG_HEADS = 32
HG_DK = 128
HG_DV = D_MODEL // HG_HEADS
HG_QK = HG_HEADS * HG_DK
HG_IN = 2 * HG_QK + 2 * HG_HEADS * HG_DV
MB_EXPAND = 2
MB_DINNER = MB_EXPAND * D_MODEL
MB_HEADDIM = 64
MB_HEADS = MB_DINNER // MB_HEADDIM
MB_DSTATE = 128
MB_GROUPS = 8
MB_CONV = 4
MB_CONV_DIM = MB_DINNER + 2 * MB_GROUPS * MB_DSTATE
MB_IN = MB_DINNER + MB_CONV_DIM + MB_HEADS
D_FF = 14336
N_EXPERTS = 8
TOP_K = 2
N_HG_LAYERS = (DEPTH + 1) // 2
N_MB_LAYERS = DEPTH // 2
N_DENSE_LAYERS = (DEPTH + 1) // 2
N_MOE_LAYERS = DEPTH // 2

kernel_name = 'hybrid_hgrn2_mamba2_stream_step'


def rmsnorm(x, g):
    xf = x.astype(jnp.float32)
    y = xf * lax.rsqrt(jnp.mean(xf * xf, axis=-1, keepdims=True) + EPS)
    return (y * g.astype(jnp.float32)).astype(x.dtype)


def _chunks(a, c):
    b, t = a.shape[0], a.shape[1]
    return jnp.moveaxis(a.reshape(b, t // c, c, *a.shape[2:]), 1, 0)


def _unchunk(a):
    n, b, c = a.shape[0], a.shape[1], a.shape[2]
    return jnp.moveaxis(a, 0, 1).reshape(b, n * c, *a.shape[3:])


def hgrn2_scan(q, k, v, logf, s0):
    t = q.shape[1]
    c = min(CHUNK, t)
    mask = jnp.tril(jnp.ones((c, c), dtype=bool))[None, :, :, None, None]

    def step(s, blk):
        qb, kb, vb, lf = blk
        b = jnp.cumsum(lf, axis=1)
        o_inter = jnp.einsum('bthk,bhkv->bthv', qb * jnp.exp(b), s)
        decay = jnp.exp(jnp.where(mask, b[:, :, None] - b[:, None, :], -jnp.inf))
        scores = jnp.einsum('bthk,btshk,bshk->bhts', qb, decay, kb)
        o_intra = jnp.einsum('bhts,bshv->bthv', scores, vb)
        b_end = b[:, -1]
        s_new = jnp.exp(b_end)[..., None] * s + jnp.einsum(
            'bshk,bshv->bhkv', kb * jnp.exp(b_end[:, None] - b), vb)
        return s_new, o_inter + o_intra

    s_fin, o = lax.scan(step, s0, tuple(_chunks(a, c) for a in (q, k, v, logf)))
    return _unchunk(o), s_fin


def ssd_scan(x, bm, cm, dt, la, s0):
    bsz, t, h, p = x.shape
    g, n = bm.shape[2], bm.shape[3]
    j = h // g
    c = min(CHUNK, t)
    mask = jnp.tril(jnp.ones((c, c), dtype=bool))[None, :, :, None]

    def step(s, blk):
        xb, bb, cb, dtb, lab = blk
        cs = jnp.cumsum(lab, axis=1)
        sg = s.reshape(bsz, g, j, p, n)
        y_inter = jnp.einsum('btgn,bgjpn->btgjp', cb, sg).reshape(bsz, c, h, p) * jnp.exp(cs)[..., None]
        cbm = jnp.einsum('btgn,bsgn->btsg', cb, bb)
        decay = jnp.exp(jnp.where(mask, cs[:, :, None] - cs[:, None, :], -jnp.inf)).reshape(bsz, c, c, g, j)
        xdt = (xb * dtb[..., None]).reshape(bsz, c, g, j, p)
        y_intra = jnp.einsum('btsgj,bsgjp->btgjp', cbm[..., None] * decay, xdt).reshape(bsz, c, h, p)
        cs_end = cs[:, -1]
        xw = (xb * (dtb * jnp.exp(cs_end[:, None] - cs))[..., None]).reshape(bsz, c, g, j, p)
        s_new = jnp.exp(cs_end)[..., None, None] * s + jnp.einsum(
            'bsgjp,bsgn->bgjpn', xw, bb).reshape(bsz, h, p, n)
        return s_new, y_inter + y_intra

    s_fin, y = lax.scan(step, s0, tuple(_chunks(a, c) for a in (x, bm, cm, dt, la)))
    return _unchunk(y), s_fin


def hgrn2_mixer(h, s0, lb, w_in, norm_g, w_out):
    bsz, t, _ = h.shape
    proj = h @ w_in
    q, fz, i, g = jnp.split(proj, [HG_QK, 2 * HG_QK, 2 * HG_QK + HG_HEADS * HG_DV], axis=-1)
    q = jax.nn.silu(q.astype(jnp.float32)).reshape(bsz, t, HG_HEADS, HG_DK)
    f = lb + (1.0 - lb) * jax.nn.sigmoid(fz.astype(jnp.float32).reshape(bsz, t, HG_HEADS, HG_DK))
    v = i.astype(jnp.float32).reshape(bsz, t, HG_HEADS, HG_DV)
    o, s_fin = hgrn2_scan(q, 1.0 - f, v, jnp.log(f), s0.astype(jnp.float32))
    o = o * lax.rsqrt(jnp.mean(o * o, axis=-1, keepdims=True) + EPS)
    o = o.reshape(bsz, t, HG_HEADS * HG_DV) * norm_g.astype(jnp.float32) * jax.nn.silu(g.astype(jnp.float32))
    return o.astype(h.dtype) @ w_out, s_fin


def mamba2_mixer(h, conv0, s0, w_in, conv_w, conv_b, dt_bias, a_log, d_skip, norm_g, w_out):
    bsz, t, _ = h.shape
    proj = h @ w_in
    z, xbc, dt = jnp.split(proj, [MB_DINNER, MB_DINNER + MB_CONV_DIM], axis=-1)
    xpad = jnp.concatenate([conv0.astype(xbc.dtype), xbc], axis=1)
    conv_new = xpad[:, xpad.shape[1] - (MB_CONV - 1):]
    xc = lax.conv_general_dilated(
        xpad, conv_w.astype(xpad.dtype)[:, None, :], window_strides=(1,), padding='VALID',
        dimension_numbers=('NWC', 'WIO', 'NWC'), feature_group_count=MB_CONV_DIM)
    xc = jax.nn.silu((xc + conv_b).astype(jnp.float32))
    xs, bm, cm = jnp.split(xc, [MB_DINNER, MB_DINNER + MB_GROUPS * MB_DSTATE], axis=-1)
    xs = xs.reshape(bsz, t, MB_HEADS, MB_HEADDIM)
    bm = bm.reshape(bsz, t, MB_GROUPS, MB_DSTATE)
    cm = cm.reshape(bsz, t, MB_GROUPS, MB_DSTATE)
    dt = jax.nn.softplus(dt.astype(jnp.float32) + dt_bias.astype(jnp.float32))
    la = dt * (-jnp.exp(a_log.astype(jnp.float32)))
    y, s_fin = ssd_scan(xs, bm, cm, dt, la, s0.astype(jnp.float32))
    y = y + d_skip.astype(jnp.float32)[:, None] * xs
    y = y.reshape(bsz, t, MB_DINNER) * jax.nn.silu(z.astype(jnp.float32))
    yg = y.reshape(bsz, t, MB_GROUPS, MB_DINNER // MB_GROUPS)
    yg = yg * lax.rsqrt(jnp.mean(yg * yg, axis=-1, keepdims=True) + EPS)
    y = yg.reshape(bsz, t, MB_DINNER) * norm_g.astype(jnp.float32)
    return y.astype(h.dtype) @ w_out, conv_new, s_fin


def swiglu(h, w1, w3, w2):
    return (jax.nn.silu(h @ w1) * (h @ w3)) @ w2


def moe_swiglu(h, w_router, w1, w3, w2):
    bsz, t, d = h.shape
    xt = h.reshape(bsz * t, d)
    logits = (xt @ w_router).astype(jnp.float32)
    top_v, top_i = lax.top_k(logits, TOP_K)
    gates = jax.nn.softmax(top_v, axis=-1)
    dense_gates = jnp.sum(jax.nn.one_hot(top_i, N_EXPERTS, dtype=jnp.float32) * gates[..., None], axis=1)
    out = jnp.zeros_like(xt)
    for e in range(N_EXPERTS):
        ye = (jax.nn.silu(xt @ w1[e]) * (xt @ w3[e])) @ w2[e]
        out = out + dense_gates[:, e:e + 1].astype(xt.dtype) * ye
    return out.reshape(bsz, t, d)


def trunk(x, hg_s0, ssm_s0, conv_s0, p):
    lbs = jnp.cumsum(jax.nn.softmax(p['hg_lb_logits'].astype(jnp.float32), axis=0), axis=0)
    hg_out, ssm_out, conv_out = [], [], []
    for i in range(DEPTH):
        j = i // N_MIXERS
        hn = rmsnorm(x, p['norm_mix'][i])
        if i % N_MIXERS == 0:
            y, s = hgrn2_mixer(hn, hg_s0[j], lbs[i].reshape(HG_HEADS, HG_DK),
                               p['hg_w_in'][j], p['hg_norm'][j], p['hg_w_out'][j])
            hg_out.append(s)
        else:
            y, cv, s = mamba2_mixer(hn, conv_s0[j], ssm_s0[j], p['mb_w_in'][j], p['mb_conv_w'][j],
                                    p['mb_conv_b'][j], p['mb_dt_bias'][j], p['mb_a_log'][j],
                                    p['mb_d'][j], p['mb_norm'][j], p['mb_w_out'][j])
            conv_out.append(cv)
            ssm_out.append(s)
        x = x + y
        hn = rmsnorm(x, p['norm_ffn'][i])
        k = i // 2
        if i % 2 == 0:
            f = swiglu(hn, p['ffn_w1'][k], p['ffn_w3'][k], p['ffn_w2'][k])
        else:
            f = moe_swiglu(hn, p['moe_router'][k], p['moe_w1'][k], p['moe_w3'][k], p['moe_w2'][k])
        x = x + f
    return rmsnorm(x, p['norm_final']), jnp.stack(hg_out), jnp.stack(ssm_out), jnp.stack(conv_out)


def setup_inputs(seed: int = 0) -> dict:
    key = jax.random.key(seed)
    ks = jax.random.split(key, 32)
    f32 = jnp.float32

    def nrm(k, shape, scale):
        return jax.random.normal(k, shape, f32) * scale

    dt0 = jnp.exp(jax.random.uniform(ks[15], (N_MB_LAYERS, MB_HEADS), f32, math.log(1e-3), math.log(1e-1)))
    return {
        'x_prompt': nrm(ks[0], (BATCH, SEQ, D_MODEL), 1.0),
        'x_sample': nrm(ks[1], (DEC_BATCH, DEC_SEQ, D_MODEL), 1.0),
        'state_hgrn': nrm(ks[2], (N_HG_LAYERS, DEC_BATCH, HG_HEADS, HG_DK, HG_DV), 0.5),
        'state_ssm': nrm(ks[3], (N_MB_LAYERS, DEC_BATCH, MB_HEADS, MB_HEADDIM, MB_DSTATE), 0.1),
        'state_conv': nrm(ks[4], (N_MB_LAYERS, DEC_BATCH, MB_CONV - 1, MB_CONV_DIM), 1.0),
        'norm_mix': 1.0 + nrm(ks[5], (DEPTH, D_MODEL), 0.05),
        'norm_ffn': 1.0 + nrm(ks[6], (DEPTH, D_MODEL), 0.05),
        'norm_final': 1.0 + nrm(ks[7], (D_MODEL,), 0.05),
        'hg_w_in': nrm(ks[8], (N_HG_LAYERS, D_MODEL, HG_IN), D_MODEL ** -0.5),
        'hg_lb_logits': nrm(ks[9], (DEPTH + 1, HG_QK), 0.1),
        'hg_norm': 1.0 + nrm(ks[10], (N_HG_LAYERS, HG_HEADS * HG_DV), 0.05),
        'hg_w_out': nrm(ks[11], (N_HG_LAYERS, HG_HEADS * HG_DV, D_MODEL), (HG_HEADS * HG_DV) ** -0.5),
        'mb_w_in': nrm(ks[12], (N_MB_LAYERS, D_MODEL, MB_IN), D_MODEL ** -0.5),
        'mb_conv_w': nrm(ks[13], (N_MB_LAYERS, MB_CONV, MB_CONV_DIM), MB_CONV ** -0.5),
        'mb_conv_b': nrm(ks[14], (N_MB_LAYERS, MB_CONV_DIM), 0.02),
        'mb_dt_bias': dt0 + jnp.log(-jnp.expm1(-dt0)),
        'mb_a_log': jnp.log(jax.random.uniform(ks[16], (N_MB_LAYERS, MB_HEADS), f32, 1.0, 16.0)),
        'mb_d': 1.0 + nrm(ks[17], (N_MB_LAYERS, MB_HEADS), 0.1),
        'mb_norm': 1.0 + nrm(ks[18], (N_MB_LAYERS, MB_DINNER), 0.05),
        'mb_w_out': nrm(ks[19], (N_MB_LAYERS, MB_DINNER, D_MODEL), MB_DINNER ** -0.5),
        'ffn_w1': nrm(ks[20], (N_DENSE_LAYERS, D_MODEL, D_FF), D_MODEL ** -0.5),
        'ffn_w3': nrm(ks[21], (N_DENSE_LAYERS, D_MODEL, D_FF), D_MODEL ** -0.5),
        'ffn_w2': nrm(ks[22], (N_DENSE_LAYERS, D_FF, D_MODEL), D_FF ** -0.5),
        'moe_router': nrm(ks[23], (N_MOE_LAYERS, D_MODEL, N_EXPERTS), D_MODEL ** -0.5),
        'moe_w1': nrm(ks[24], (N_MOE_LAYERS, N_EXPERTS, D_MODEL, D_FF), D_MODEL ** -0.5),
        'moe_w3': nrm(ks[25], (N_MOE_LAYERS, N_EXPERTS, D_MODEL, D_FF), D_MODEL ** -0.5),
        'moe_w2': nrm(ks[26], (N_MOE_LAYERS, N_EXPERTS, D_FF, D_MODEL), D_FF ** -0.5),
    }


def reference(x_prompt, x_sample, state_hgrn, state_ssm, state_conv, norm_mix, norm_ffn, norm_final,
              hg_w_in, hg_lb_logits, hg_norm, hg_w_out, mb_w_in, mb_conv_w, mb_conv_b, mb_dt_bias,
              mb_a_log, mb_d, mb_norm, mb_w_out, ffn_w1, ffn_w3, ffn_w2, moe_router, moe_w1, moe_w3,
              moe_w2):
    p = dict(norm_mix=norm_mix, norm_ffn=norm_ffn, norm_final=norm_final, hg_w_in=hg_w_in,
             hg_lb_logits=hg_lb_logits, hg_norm=hg_norm, hg_w_out=hg_w_out, mb_w_in=mb_w_in,
             mb_conv_w=mb_conv_w, mb_conv_b=mb_conv_b, mb_dt_bias=mb_dt_bias, mb_a_log=mb_a_log,
             mb_d=mb_d, mb_norm=mb_norm, mb_w_out=mb_w_out, ffn_w1=ffn_w1, ffn_w3=ffn_w3, ffn_w2=ffn_w2,
             moe_router=moe_router, moe_w1=moe_w1, moe_w3=moe_w3, moe_w2=moe_w2)
    hg0 = jnp.zeros((N_HG_LAYERS, BATCH, HG_HEADS, HG_DK, HG_DV), jnp.float32)
    ssm0 = jnp.zeros((N_MB_LAYERS, BATCH, MB_HEADS, MB_HEADDIM, MB_DSTATE), jnp.float32)
    conv0 = jnp.zeros((N_MB_LAYERS, BATCH, MB_CONV - 1, MB_CONV_DIM), x_prompt.dtype)
    y_prompt, hgrn_prompt, ssm_prompt, conv_prompt = trunk(x_prompt, hg0, ssm0, conv0, p)
    y_sample, hgrn_sample, ssm_sample, conv_sample = trunk(x_sample, state_hgrn, state_ssm, state_conv, p)
    return (y_prompt, y_sample, hgrn_prompt, hgrn_sample, ssm_prompt, ssm_sample, conv_prompt, conv_sample)
```

```python
import functools

import jax
import jax.numpy as jnp
from jax import lax
from jax.experimental import pallas as pl
from jax.experimental.pallas import tpu as pltpu

EPS = 1e-5
CHUNK = 64
SUB = 16
MB_GROUPS = 8
MB_HEADDIM = 64
MB_DSTATE = 128
MB_CONV = 4
TOP_K = 2
NEG_BIG = -1e30

V7X_VMEM_BYTES = 64 * 1024 * 1024
VMEM_LIMIT = V7X_VMEM_BYTES - 8 * 1024 * 1024
LANES = 128

F32 = jnp.float32
BF16 = jnp.bfloat16


def _cparams(sem):
    return pltpu.CompilerParams(dimension_semantics=sem, vmem_limit_bytes=VMEM_LIMIT)


def _sigmoid(x):
    return 1.0 / (1.0 + jnp.exp(-x))


def _silu(x):
    return x * _sigmoid(x)


def _split3(x):
    hi = x.astype(BF16)
    r = x - hi.astype(F32)
    mid = r.astype(BF16)
    lo = (r - mid.astype(F32)).astype(BF16)
    return hi, mid, lo


def _dot(a, b):
    return jnp.dot(a, b, preferred_element_type=F32)


def _dot_nt(a, b):
    return lax.dot_general(a, b, (((1,), (1,)), ((), ())), preferred_element_type=F32)


def _dot_tn(a, b):
    return lax.dot_general(a, b, (((0,), (0,)), ((), ())), preferred_element_type=F32)


def _rmsnorm_kernel(x_ref, g_ref, o_ref):
    x = x_ref[...]
    ms = jnp.mean(x * x, axis=-1, keepdims=True)
    o_ref[...] = (x * lax.rsqrt(ms + EPS) * g_ref[...]).astype(o_ref.dtype)


def rmsnorm(x, g, out_dtype, tr=256):
    m, d = x.shape
    return pl.pallas_call(
        _rmsnorm_kernel,
        grid=(m // tr,),
        in_specs=[pl.BlockSpec((tr, d), lambda i: (i, 0)), pl.BlockSpec((1, d), lambda i: (0, 0))],
        out_specs=pl.BlockSpec((tr, d), lambda i: (i, 0)),
        out_shape=jax.ShapeDtypeStruct((m, d), out_dtype),
        compiler_params=_cparams(("parallel",)),
        name="rmsnorm",
    )(x, g.reshape(1, d))


def _rmsnorm_router_kernel(x_ref, g_ref, wr_ref, o_ref, idx_ref, gate_ref, *, n_experts):
    x = x_ref[...]
    ms = jnp.mean(x * x, axis=-1, keepdims=True)
    hn = x * lax.rsqrt(ms + EPS) * g_ref[...]
    o_ref[...] = hn
    logits = jnp.dot(hn, wr_ref[...], preferred_element_type=F32, precision=lax.Precision.HIGHEST)
    lane = lax.broadcasted_iota(jnp.int32, logits.shape, 1).astype(F32)
    logits = jnp.where(lane < n_experts, logits, -jnp.inf)
    m1 = jnp.max(logits, axis=-1, keepdims=True)
    i1 = jnp.min(jnp.where(logits == m1, lane, float(LANES)), axis=-1, keepdims=True)
    rest = jnp.where(lane == i1, -jnp.inf, logits)
    m2 = jnp.max(rest, axis=-1, keepdims=True)
    i2 = jnp.min(jnp.where(rest == m2, lane, float(LANES)), axis=-1, keepdims=True)
    e2 = jnp.exp(m2 - m1)
    g1 = 1.0 / (1.0 + e2)
    g2 = e2 / (1.0 + e2)
    idx_ref[...] = jnp.where(lane == 0.0, i1, jnp.where(lane == 1.0, i2, 0.0)).astype(jnp.int32)
    gate_ref[...] = jnp.where(lane == 0.0, g1, jnp.where(lane == 1.0, g2, 0.0))


def rmsnorm_router(x, g, w_router, tr=256):
    m, d = x.shape
    n_experts = w_router.shape[1]
    wr = jnp.zeros((d, LANES), F32).at[:, :n_experts].set(w_router)
    hn, idx, gate = pl.pallas_call(
        functools.partial(_rmsnorm_router_kernel, n_experts=n_experts),
        grid=(m // tr,),
        in_specs=[pl.BlockSpec((tr, d), lambda i: (i, 0)), pl.BlockSpec((1, d), lambda i: (0, 0)),
                  pl.BlockSpec((d, LANES), lambda i: (0, 0))],
        out_specs=[pl.BlockSpec((tr, d), lambda i: (i, 0)), pl.BlockSpec((tr, LANES), lambda i: (i, 0)),
                   pl.BlockSpec((tr, LANES), lambda i: (i, 0))],
        out_shape=[jax.ShapeDtypeStruct((m, d), F32), jax.ShapeDtypeStruct((m, LANES), jnp.int32),
                   jax.ShapeDtypeStruct((m, LANES), F32)],
        compiler_params=_cparams(("parallel",)),
        name="rmsnorm_router",
    )(x, g.reshape(1, d), wr)
    return hn, idx[:, :TOP_K], gate[:, :TOP_K]


def _mm_kernel(*refs, nk, mode):
    n_w = 2 if mode == "swiglu" else 1
    a_ref = refs[0]
    w_refs = refs[1:1 + n_w]
    pos = 1 + n_w
    res_ref = None
    if mode == "res":
        res_ref = refs[pos]
        pos += 1
    o_ref = refs[pos]
    acc_refs = refs[pos + 1:]

    def finish(accs):
        if mode == "swiglu":
            o_ref[...] = (_silu(accs[0]) * accs[1]).astype(o_ref.dtype)
        elif mode == "res":
            o_ref[...] = res_ref[...] + accs[0]
        else:
            o_ref[...] = accs[0].astype(o_ref.dtype)

    a = a_ref[...]
    parts = [_dot(a, w[...]) for w in w_refs]
    if nk == 1:
        finish(parts)
        return
    k = pl.program_id(2)

    @pl.when(k == 0)
    def _():
        for acc, p in zip(acc_refs, parts):
            acc[...] = p

    @pl.when(k > 0)
    def _():
        for acc, p in zip(acc_refs, parts):
            acc[...] += p

    @pl.when(k == nk - 1)
    def _():
        finish([acc[...] for acc in acc_refs])


def matmul(a, ws, *, mode="plain", res=None, out_dtype=F32, n_out=None, col_off=0, tm, tn, tk, name):
    m, kdim = a.shape
    n_out = ws[0].shape[1] if n_out is None else n_out
    nk = kdim // tk
    grid = (m // tm, n_out // tn, nk)
    in_specs = [pl.BlockSpec((tm, tk), lambda i, j, k: (i, k))]
    in_specs += [pl.BlockSpec((tk, tn), lambda i, j, k: (k, j + col_off)) for _ in ws]
    args = [a, *ws]
    if mode == "res":
        in_specs.append(pl.BlockSpec((tm, tn), lambda i, j, k: (i, j)))
        args.append(res)
    scratch = [pltpu.VMEM((tm, tn), F32) for _ in ws] if nk > 1 else []
    return pl.pallas_call(
        functools.partial(_mm_kernel, nk=nk, mode=mode),
        grid=grid,
        in_specs=in_specs,
        out_specs=pl.BlockSpec((tm, tn), lambda i, j, k: (i, j)),
        out_shape=jax.ShapeDtypeStruct((m, n_out), out_dtype),
        scratch_shapes=scratch,
        compiler_params=_cparams(("parallel", "parallel", "arbitrary")),
        name=name,
    )(*args)


def _hgrn_chunk(q_raw, fz, v, g_raw, lb, ng, st, c):
    q = _silu(q_raw)
    f = lb + (1.0 - lb) * _sigmoid(fz)
    lf = jnp.log(f)
    k = 1.0 - f
    row = lax.broadcasted_iota(jnp.int32, (c, c), 0)
    col = lax.broadcasted_iota(jnp.int32, (c, c), 1)
    tri = (row >= col).astype(BF16)
    hi, mid, lo = _split3(lf)
    b = _dot(tri, hi) + _dot(tri, mid) + _dot(tri, lo)
    b_end = b[c - 1:c, :]

    o = _dot_nt((q * jnp.exp(b)).astype(BF16), st.astype(BF16))
    rows16 = lax.broadcasted_iota(jnp.int32, (SUB, 1), 0)
    outs = []
    for i in range(c // SUB):
        lo_r, hi_r = i * SUB, (i + 1) * SUB
        qi, ki, vi, bi = q[lo_r:hi_r], k[lo_r:hi_r], v[lo_r:hi_r], b[lo_r:hi_r]
        oi = o[lo_r:hi_r]
        for s in range(SUB):
            e = jnp.exp(jnp.where(rows16 >= s, bi - bi[s:s + 1, :], NEG_BIG))
            w = jnp.sum(qi * e * ki[s:s + 1, :], axis=-1, keepdims=True)
            oi = oi + w * vi[s:s + 1, :]
        if i > 0:
            anchor = b[lo_r - 1:lo_r, :]
            qt = (qi * jnp.exp(bi - anchor)).astype(BF16)
            kt = (k[:lo_r] * jnp.exp(anchor - b[:lo_r])).astype(BF16)
            sc = _dot_nt(qt, kt)
            oi = oi + _dot(sc.astype(BF16), v[:lo_r].astype(BF16))
        outs.append(oi)
    o = outs[0] if len(outs) == 1 else jnp.concatenate(outs, axis=0)

    kd = (k * jnp.exp(b_end - b)).astype(BF16)
    st_new = st * jnp.exp(b_end) + _dot_tn(v.astype(BF16), kd)

    o = o * lax.rsqrt(jnp.mean(o * o, axis=-1, keepdims=True) + EPS)
    return o * ng * _silu(g_raw), st_new


def _hgrn_prompt_kernel(q_ref, f_ref, v_ref, g_ref, lb_ref, ng_ref, o_ref, s_ref, st_ref, *, hb, c, dk):
    t = q_ref.shape[0]
    st_ref[...] = jnp.zeros_like(st_ref)

    def body(ci, carry):
        r0 = pl.multiple_of(ci * c, c)
        for h in range(hb):
            ls = slice(h * dk, (h + 1) * dk)
            o, st_new = _hgrn_chunk(q_ref[pl.ds(r0, c), ls], f_ref[pl.ds(r0, c), ls], v_ref[pl.ds(r0, c), ls],
                                    g_ref[pl.ds(r0, c), ls], lb_ref[:, ls], ng_ref[:, ls], st_ref[h], c)
            st_ref[h] = st_new
            o_ref[pl.ds(r0, c), ls] = o.astype(o_ref.dtype)
        return carry

    lax.fori_loop(0, t // c, body, 0)
    for h in range(hb):
        s_ref[0, h] = st_ref[h].T


def _hgrn_sample_kernel(q_ref, f_ref, v_ref, g_ref, lb_ref, ng_ref, s0_ref, o_ref, s_ref, *, hb, c, dk):
    nb = s0_ref.shape[0]

    def body(bi, carry):
        r0 = pl.multiple_of(bi * c, c)
        for h in range(hb):
            ls = slice(h * dk, (h + 1) * dk)
            o, st_new = _hgrn_chunk(q_ref[pl.ds(r0, c), ls], f_ref[pl.ds(r0, c), ls], v_ref[pl.ds(r0, c), ls],
                                    g_ref[pl.ds(r0, c), ls], lb_ref[:, ls], ng_ref[:, ls], s0_ref[bi, h].T, c)
            s_ref[bi, h] = st_new.T
            o_ref[pl.ds(r0, c), ls] = o.astype(o_ref.dtype)
        return carry

    lax.fori_loop(0, nb, body, 0)


def hgrn_scan(proj, lb, ng, s0, *, row0, nbatch, t, heads, dk, hb, out_rows):
    w = hb * dk
    nhb = heads // hb
    lb2 = lb.reshape(1, heads * dk)
    ng2 = ng.reshape(1, heads * dk)
    if s0 is None:
        c = min(CHUNK, t)
        rb0 = row0 // t
        kern = functools.partial(_hgrn_prompt_kernel, hb=hb, c=c, dk=dk)
        col = lambda part: pl.BlockSpec((t, w), lambda b, j, part=part: (rb0 + b, part * nhb + j))
        vec = pl.BlockSpec((1, w), lambda b, j: (0, j))
        return pl.pallas_call(
            kern,
            grid=(nbatch, nhb),
            in_specs=[col(0), col(1), col(2), col(3), vec, vec],
            out_specs=[pl.BlockSpec((t, w), lambda b, j: (b, j)),
                       pl.BlockSpec((1, hb, dk, dk), lambda b, j: (b, j, 0, 0))],
            out_shape=[jax.ShapeDtypeStruct((nbatch * t, heads * dk), BF16),
                       jax.ShapeDtypeStruct((nbatch, heads, dk, dk), F32)],
            scratch_shapes=[pltpu.VMEM((hb, dk, dk), F32)],
            compiler_params=_cparams(("parallel", "parallel")),
            name="hgrn_prompt",
        )(proj, proj, proj, proj, lb2, ng2)
    rows = nbatch * t
    rb0 = row0 // rows
    kern = functools.partial(_hgrn_sample_kernel, hb=hb, c=t, dk=dk)
    col = lambda part: pl.BlockSpec((rows, w), lambda j, part=part: (rb0, part * nhb + j))
    vec = pl.BlockSpec((1, w), lambda j: (0, j))
    return pl.pallas_call(
        kern,
        grid=(nhb,),
        in_specs=[col(0), col(1), col(2), col(3), vec, vec,
                  pl.BlockSpec((nbatch, hb, dk, dk), lambda j: (0, j, 0, 0))],
        out_specs=[pl.BlockSpec((rows, w), lambda j: (0, j)),
                   pl.BlockSpec((nbatch, hb, dk, dk), lambda j: (0, j, 0, 0))],
        out_shape=[jax.ShapeDtypeStruct((rows, heads * dk), BF16),
                   jax.ShapeDtypeStruct((nbatch, heads, dk, dk), F32)],
        compiler_params=_cparams(("parallel",)),
        name="hgrn_sample",
    )(proj, proj, proj, proj, lb2, ng2, s0)


def _conv_silu(x, pad_ref, w_ref, bias_ref, c, first):
    pad_ref[pl.ds(8, c), :] = x
    acc = bias_ref[...] + w_ref[MB_CONV - 1:MB_CONV, :] * x
    for j in range(MB_CONV - 1):
        acc = acc + w_ref[j:j + 1, :] * pad_ref[pl.ds(5 + j, c), :]
    pad_ref[pl.ds(5, 3), :] = pad_ref[pl.ds(5 + c, 3), :]
    return _silu(acc)


def _ssd_kernel(z_ref, x_ref, b_ref, c_ref, dt_ref, *rest, c, heads, has_state):
    if has_state:
        conv0x_ref, conv0b_ref, conv0c_ref, s0_ref = rest[:4]
        rest = rest[4:]
    (wx_ref, wb_ref, wc_ref, bx_ref, bb_ref, bc_ref, dtb_ref, alog_ref, dsk_ref, ng_ref, ep_ref, ec_ref,
     y_ref, s_ref, st_ref, padx_ref, padb_ref, padc_ref) = rest
    p = MB_HEADDIM
    n = MB_DSTATE
    hpg = heads // MB_GROUPS
    gw = hpg * p
    ci = pl.program_id(1)
    nci = pl.num_programs(1)

    @pl.when(ci == 0)
    def _():
        if has_state:
            padx_ref[pl.ds(5, 3), :] = conv0x_ref[0]
            padb_ref[pl.ds(5, 3), :] = conv0b_ref[0]
            padc_ref[pl.ds(5, 3), :] = conv0c_ref[0]
            st_ref[...] = s0_ref[0].T
        else:
            padx_ref[pl.ds(5, 3), :] = jnp.zeros((3, padx_ref.shape[1]), F32)
            padb_ref[pl.ds(5, 3), :] = jnp.zeros((3, padb_ref.shape[1]), F32)
            padc_ref[pl.ds(5, 3), :] = jnp.zeros((3, padc_ref.shape[1]), F32)
            st_ref[...] = jnp.zeros_like(st_ref)

    bm = _conv_silu(b_ref[...], padb_ref, wb_ref, bb_ref, c, None)
    cm = _conv_silu(c_ref[...], padc_ref, wc_ref, bc_ref, c, None)

    dtr = dt_ref[...] + dtb_ref[...]
    dt = jnp.maximum(dtr, 0.0) + jnp.log1p(jnp.exp(-jnp.abs(dtr)))
    la = dt * (-jnp.exp(alog_ref[...]))
    row = lax.broadcasted_iota(jnp.int32, (c, c), 0)
    col = lax.broadcasted_iota(jnp.int32, (c, c), 1)
    tri = (row >= col).astype(BF16)
    hi, mid, lo = _split3(la)
    cs = _dot(tri, hi) + _dot(tri, mid) + _dot(tri, lo)
    cs3 = _split3(cs)
    dt3 = _split3(dt)

    mw = hpg * c
    t_m = lax.broadcasted_iota(jnp.int32, (c, mw), 0)
    s_m = lax.broadcasted_iota(jnp.int32, (c, mw), 1) % c
    nbk = LANES // c
    bd_rows = lax.broadcasted_iota(jnp.int32, (LANES, nbk * p), 0) // c
    bd_cols = lax.broadcasted_iota(jnp.int32, (LANES, nbk * p), 1) // p
    bd_mask = bd_rows == bd_cols

    for g in range(MB_GROUPS):
        xl = slice(g * gw, (g + 1) * gw)
        nl = slice(g * n, (g + 1) * n)
        hl = slice(g * hpg, (g + 1) * hpg)
        xs = _conv_silu(x_ref[:, xl], padx_ref.at[:, xl], wx_ref.at[:, xl], bx_ref.at[:, xl], c, None)
        bg = bm[:, nl]
        cg = cm[:, nl]
        ep = ep_ref[:, xl]
        cse = _dot(cs3[0], ep) + _dot(cs3[1], ep) + _dot(cs3[2], ep)
        dte = _dot(dt3[0], ep) + _dot(dt3[1], ep) + _dot(dt3[2], ep)
        cs_end = cse[c - 1:c, :]
        xdt = xs * dte
        xw = xdt * jnp.exp(cs_end - cse)

        st_g = st_ref[:, xl]
        y = _dot(cg.astype(BF16), st_g.astype(BF16)) * jnp.exp(cse)
        st_ref[:, xl] = st_g * jnp.exp(cs_end) + _dot_tn(bg.astype(BF16), xw.astype(BF16))

        ec = ec_ref[:, g * mw:(g + 1) * mw]
        csm = _dot(cs3[0], ec) + _dot(cs3[1], ec) + _dot(cs3[2], ec)
        cs_s = jnp.sum(jnp.where(t_m == s_m, csm, 0.0), axis=0, keepdims=True)
        lw = jnp.exp(jnp.where(t_m >= s_m, csm - cs_s, NEG_BIG))
        b_t = jnp.concatenate([bg] * hpg, axis=0).astype(BF16)
        mm = (_dot_nt(cg.astype(BF16), b_t) * lw).astype(BF16)
        xdt_b = xdt.astype(BF16)
        parts = []
        for blk in range(mw // LANES):
            xb = xdt_b[:, blk * nbk * p:(blk + 1) * nbk * p]
            bd = jnp.where(bd_mask, jnp.concatenate([xb] * nbk, axis=0), jnp.zeros((), BF16))
            parts.append(_dot(mm[:, blk * LANES:(blk + 1) * LANES], bd))
        y = y + jnp.concatenate(parts, axis=1) + dsk_ref[:, xl] * xs

        y = y * _silu(z_ref[:, xl])
        y = y * lax.rsqrt(jnp.mean(y * y, axis=-1, keepdims=True) + EPS)
        y_ref[:, xl] = (y * ng_ref[:, xl]).astype(y_ref.dtype)

    @pl.when(ci == nci - 1)
    def _():
        s_ref[0] = st_ref[...].T


def ssd_scan(zx, dt_raw, conv0, s0, conv_w, conv_b, dt_bias, a_log, d_skip, norm_g, *, row0, nbatch, t, heads):
    p, n = MB_HEADDIM, MB_DSTATE
    di = heads * p
    gn = MB_GROUPS * n
    c = min(CHUNK, t)
    nch = t // c
    rb0 = row0 // c
    has_state = s0 is not None
    f32 = F32
    head_of_x = jnp.arange(di) // p
    ep = (jnp.arange(heads)[:, None] == head_of_x[None, :]).astype(BF16)
    head_of_m = jnp.arange(heads * c) // c
    ec = (jnp.arange(heads)[:, None] == head_of_m[None, :]).astype(BF16)
    dsk = jnp.repeat(d_skip.astype(f32), p).reshape(1, di)

    row = lambda w, blk: pl.BlockSpec((c, w), lambda b, k, blk=blk: (rb0 + b * nch + k, blk))
    full = lambda a: pl.BlockSpec(a.shape, lambda b, k: (0,) * a.ndim)
    in_specs = [row(di, 0), row(di, 1), row(gn, 2 * di // gn), row(gn, 2 * di // gn + 1), row(heads, 0)]
    args = [zx, zx, zx, zx, dt_raw]
    if has_state:
        in_specs += [pl.BlockSpec((1, MB_CONV - 1, di), lambda b, k: (b, 0, 0)),
                     pl.BlockSpec((1, MB_CONV - 1, gn), lambda b, k: (b, 0, 0)),
                     pl.BlockSpec((1, MB_CONV - 1, gn), lambda b, k: (b, 0, 0)),
                     pl.BlockSpec((1, di, n), lambda b, k: (b, 0, 0))]
        args += [conv0[:, :, :di], conv0[:, :, di:di + gn], conv0[:, :, di + gn:], s0.reshape(nbatch, di, n)]
    consts = [conv_w[:, :di], conv_w[:, di:di + gn], conv_w[:, di + gn:],
              conv_b[:di].reshape(1, di), conv_b[di:di + gn].reshape(1, gn), conv_b[di + gn:].reshape(1, gn),
              dt_bias.reshape(1, heads), a_log.reshape(1, heads), dsk, norm_g.reshape(1, di), ep, ec]
    in_specs += [full(a) for a in consts]
    args += consts
    y, s_fin = pl.pallas_call(
        functools.partial(_ssd_kernel, c=c, heads=heads, has_state=has_state),
        grid=(nbatch, nch),
        in_specs=in_specs,
        out_specs=[pl.BlockSpec((c, di), lambda b, k: (b * nch + k, 0)),
                   pl.BlockSpec((1, di, n), lambda b, k: (b, 0, 0))],
        out_shape=[jax.ShapeDtypeStruct((nbatch * t, di), BF16), jax.ShapeDtypeStruct((nbatch, di, n), f32)],
        scratch_shapes=[pltpu.VMEM((n, di), f32), pltpu.VMEM((c + 8, di), f32),
                        pltpu.VMEM((c + 8, gn), f32), pltpu.VMEM((c + 8, gn), f32)],
        compiler_params=_cparams(("parallel", "arbitrary")),
        name="ssd_state" if has_state else "ssd_prompt",
    )(*args)
    return y, s_fin.reshape(nbatch, heads, p, n)


def _gather_rows_kernel(idx_ref, src_ref, o_ref, sem, *, tg):
    i = pl.program_id(0)

    def copy(r):
        return pltpu.make_async_copy(src_ref.at[pl.ds(idx_ref[i * tg + r], 1)], o_ref.at[pl.ds(r, 1)], sem)

    def start(r, carry):
        copy(r).start()
        return carry

    def wait(r, carry):
        copy(r).wait()
        return carry

    lax.fori_loop(0, tg, start, 0)
    lax.fori_loop(0, tg, wait, 0)


def gather_rows(src, idx, tg=256):
    rows = idx.shape[0]
    d = src.shape[1]
    return pl.pallas_call(
        functools.partial(_gather_rows_kernel, tg=tg),
        grid_spec=pltpu.PrefetchScalarGridSpec(
            num_scalar_prefetch=1,
            grid=(rows // tg,),
            in_specs=[pl.BlockSpec(memory_space=pl.ANY)],
            out_specs=pl.BlockSpec((tg, d), lambda i, idx: (i, 0)),
            scratch_shapes=[pltpu.SemaphoreType.DMA(())],
        ),
        out_shape=jax.ShapeDtypeStruct((rows, d), src.dtype),
        compiler_params=_cparams(("arbitrary",)),
        name="moe_gather",
    )(idx, src)


def _moe_ffn1_kernel(te_ref, nu_ref, a_ref, w1_ref, w3_ref, o_ref):
    used = pl.program_id(1) < nu_ref[0]

    @pl.when(used)
    def _():
        a = a_ref[...].astype(BF16)
        o_ref[...] = (_silu(_dot(a, w1_ref[0])) * _dot(a, w3_ref[0])).astype(o_ref.dtype)

    @pl.when(jnp.logical_not(used))
    def _():
        o_ref[...] = jnp.zeros_like(o_ref)


def moe_ffn1(xs, w1, w3, tile_expert, n_used, *, tmg, tn):
    rp, d = xs.shape
    dff = w1.shape[2]
    nt = rp // tmg

    def a_map(j, i, te, nu):
        return (jnp.minimum(i, nu[0] - 1), 0)

    def w_map(j, i, te, nu):
        return (te[i], 0, j)

    def o_map(j, i, te, nu):
        return (i, j)

    return pl.pallas_call(
        _moe_ffn1_kernel,
        grid_spec=pltpu.PrefetchScalarGridSpec(
            num_scalar_prefetch=2,
            grid=(dff // tn, nt),
            in_specs=[pl.BlockSpec((tmg, d), a_map), pl.BlockSpec((1, d, tn), w_map),
                      pl.BlockSpec((1, d, tn), w_map)],
            out_specs=pl.BlockSpec((tmg, tn), o_map),
        ),
        out_shape=jax.ShapeDtypeStruct((rp, dff), BF16),
        compiler_params=_cparams(("parallel", "arbitrary")),
        name="moe_ffn1",
    )(tile_expert, n_used, xs, w1, w3)


def _moe_ffn2_kernel(te_ref, nu_ref, h_ref, w2_ref, gate_ref, o_ref):
    used = pl.program_id(1) < nu_ref[0]

    @pl.when(used)
    def _():
        o_ref[...] = _dot(h_ref[...], w2_ref[0]) * gate_ref[...]

    @pl.when(jnp.logical_not(used))
    def _():
        o_ref[...] = jnp.zeros_like(o_ref)


def moe_ffn2(h, w2, row_gate, tile_expert, n_used, *, tmg, tn):
    rp, dff = h.shape
    d = w2.shape[2]
    nt = rp // tmg

    def a_map(j, i, te, nu):
        return (jnp.minimum(i, nu[0] - 1), 0)

    def w_map(j, i, te, nu):
        return (te[i], 0, j)

    def o_map(j, i, te, nu):
        return (i, j)

    return pl.pallas_call(
        _moe_ffn2_kernel,
        grid_spec=pltpu.PrefetchScalarGridSpec(
            num_scalar_prefetch=2,
            grid=(d // tn, nt),
            in_specs=[pl.BlockSpec((tmg, dff), a_map), pl.BlockSpec((1, dff, tn), w_map),
                      pl.BlockSpec((tmg, 1), a_map)],
            out_specs=pl.BlockSpec((tmg, tn), o_map),
        ),
        out_shape=jax.ShapeDtypeStruct((rp, d), F32),
        compiler_params=_cparams(("parallel", "arbitrary")),
        name="moe_ffn2",
    )(tile_expert, n_used, h, w2, row_gate)


def _combine_norm_kernel(dest_ref, x_ref, y_ref, g_ref, o_ref, buf_ref, sem, *, tc):
    i = pl.program_id(0)

    def copy(r):
        row = dest_ref[i * TOP_K * tc + r]
        return pltpu.make_async_copy(y_ref.at[pl.ds(row, 1)], buf_ref.at[pl.ds(r, 1)], sem)

    def start(r, carry):
        copy(r).start()
        return carry

    def wait(r, carry):
        copy(r).wait()
        return carry

    lax.fori_loop(0, TOP_K * tc, start, 0)
    lax.fori_loop(0, TOP_K * tc, wait, 0)
    x = x_ref[...]
    for kk in range(TOP_K):
        x = x + buf_ref[pl.ds(kk * tc, tc), :]
    ms = jnp.mean(x * x, axis=-1, keepdims=True)
    o_ref[...] = x * lax.rsqrt(ms + EPS) * g_ref[...]


def combine_norm(x, y_rows, dest, g, tc=128):
    m, d = x.shape
    return pl.pallas_call(
        functools.partial(_combine_norm_kernel, tc=tc),
        grid_spec=pltpu.PrefetchScalarGridSpec(
            num_scalar_prefetch=1,
            grid=(m // tc,),
            in_specs=[pl.BlockSpec((tc, d), lambda i, dest: (i, 0)), pl.BlockSpec(memory_space=pl.ANY),
                      pl.BlockSpec((1, d), lambda i, dest: (0, 0))],
            out_specs=pl.BlockSpec((tc, d), lambda i, dest: (i, 0)),
            scratch_shapes=[pltpu.VMEM((TOP_K * tc, d), F32), pltpu.SemaphoreType.DMA(())],
        ),
        out_shape=jax.ShapeDtypeStruct((m, d), F32),
        compiler_params=_cparams(("arbitrary",)),
        name="moe_combine_norm",
    )(dest, x, y_rows, g.reshape(1, d))


def moe_dispatch(top_i, gates, n_experts, tmg):
    m = top_i.shape[0]
    na = m * TOP_K
    nt = na // tmg + n_experts
    flat_e = top_i.reshape(na).astype(jnp.int32)
    order = jnp.argsort(flat_e, stable=True).astype(jnp.int32)
    counts = jnp.zeros((n_experts,), jnp.int32).at[flat_e].add(1)
    padded = (counts + tmg - 1) // tmg * tmg
    pad_end = jnp.cumsum(padded)
    pad_start = pad_end - padded
    cnt_start = jnp.cumsum(counts) - counts
    sorted_e = flat_e[order]
    dest_sorted = pad_start[sorted_e] + jnp.arange(na, dtype=jnp.int32) - cnt_start[sorted_e]
    row_token = jnp.zeros((nt * tmg,), jnp.int32).at[dest_sorted].set(order // TOP_K)
    row_gate = jnp.zeros((nt * tmg,), F32).at[dest_sorted].set(gates.reshape(na)[order])
    dest = jnp.zeros((na,), jnp.int32).at[order].set(dest_sorted)
    n_used = (pad_end[-1] // tmg).astype(jnp.int32).reshape(1)
    tile_start = jnp.arange(nt, dtype=jnp.int32) * tmg
    tile_expert = jnp.searchsorted(pad_end, tile_start, side="right").astype(jnp.int32)
    last_expert = tile_expert[jnp.maximum(n_used[0] - 1, 0)]
    tile_expert = jnp.where(tile_start < pad_end[-1], tile_expert, last_expert)
    return row_token, row_gate.reshape(-1, 1), dest, tile_expert, n_used


def kernel(x_prompt, x_sample, state_hgrn, state_ssm, state_conv, norm_mix, norm_ffn, norm_final, hg_w_in, hg_lb_logits, hg_norm, hg_w_out, mb_w_in, mb_conv_w, mb_conv_b, mb_dt_bias, mb_a_log, mb_d, mb_norm, mb_w_out, ffn_w1, ffn_w3, ffn_w2, moe_router, moe_w1, moe_w3, moe_w2):
    bp, tp, d = x_prompt.shape
    bs, ts, _ = x_sample.shape
    mp, ms = bp * tp, bs * ts
    m = mp + ms
    hg_heads, hg_dk = state_hgrn.shape[2], state_hgrn.shape[3]
    mb_heads = state_ssm.shape[2]
    di = mb_heads * MB_HEADDIM
    conv_dim = state_conv.shape[3]
    n_experts = moe_router.shape[2]
    tm = m // 8

    x = jnp.concatenate([x_prompt.reshape(mp, d), x_sample.reshape(ms, d)], axis=0)
    bf = lambda w: w.astype(BF16)

    lbs = jnp.cumsum(jax.nn.softmax(hg_lb_logits.astype(F32), axis=0), axis=0)
    hn = rmsnorm(x, norm_mix[0], BF16)
    proj = matmul(hn, [bf(hg_w_in[0])], tm=tm, tn=1024, tk=d, name="hg_in")
    scan = functools.partial(hgrn_scan, proj, lbs[0], hg_norm[0], heads=hg_heads, dk=hg_dk, hb=2, out_rows=m)
    o_p, hg_p = scan(None, row0=0, nbatch=bp, t=tp)
    o_s, hg_s = scan(state_hgrn[0], row0=mp, nbatch=bs, t=ts)
    o = jnp.concatenate([o_p, o_s], axis=0)
    x = matmul(o, [bf(hg_w_out[0])], mode="res", res=x, tm=tm, tn=1024, tk=2048, name="hg_out")
    hn = rmsnorm(x, norm_ffn[0], BF16)
    h = matmul(hn, [bf(ffn_w1[0]), bf(ffn_w3[0])], mode="swiglu", out_dtype=BF16, tm=tm, tn=512, tk=d, name="ffn_in")
    x = matmul(h, [bf(ffn_w2[0])], mode="res", res=x, tm=tm, tn=1024, tk=2048, name="ffn_out")

    hn = rmsnorm(x, norm_mix[1], BF16)
    w_in = bf(mb_w_in[0])
    zx = matmul(hn, [w_in], n_out=di + conv_dim, tm=tm, tn=1024, tk=d, name="mb_in")
    dt_raw = matmul(hn, [w_in], n_out=mb_heads, col_off=(di + conv_dim) // mb_heads, tm=tm, tn=mb_heads, tk=d,
                    name="mb_in_dt")
    ssd = functools.partial(ssd_scan, zx, dt_raw, conv_w=mb_conv_w[0], conv_b=mb_conv_b[0], dt_bias=mb_dt_bias[0],
                            a_log=mb_a_log[0], d_skip=mb_d[0], norm_g=mb_norm[0], heads=mb_heads)
    y_p, ssm_p = ssd(None, None, row0=0, nbatch=bp, t=tp)
    y_s, ssm_s = ssd(state_conv[0], state_ssm[0], row0=mp, nbatch=bs, t=ts)
    xbc = zx[:, di:]
    conv_p = xbc[:mp].reshape(bp, tp, conv_dim)[:, tp - (MB_CONV - 1):]
    conv_s = xbc[mp:].reshape(bs, ts, conv_dim)[:, ts - (MB_CONV - 1):]
    y = jnp.concatenate([y_p, y_s], axis=0)
    x = matmul(y, [bf(mb_w_out[0])], mode="res", res=x, tm=tm, tn=1024, tk=2048, name="mb_out")

    tmg = 256
    hn32, top_i, gates = rmsnorm_router(x, norm_ffn[1], moe_router[0])
    row_token, row_gate, dest, tile_expert, n_used = moe_dispatch(top_i, gates, n_experts, tmg)
    xs = gather_rows(hn32, row_token)
    hmid = moe_ffn1(xs, bf(moe_w1[0]), bf(moe_w3[0]), tile_expert, n_used, tmg=tmg, tn=1024)
    yr = moe_ffn2(hmid, bf(moe_w2[0]), row_gate, tile_expert, n_used, tmg=tmg, tn=512)
    tc = 128
    dest_tiles = dest.reshape(m // tc, tc, TOP_K).transpose(0, 2, 1).reshape(-1)
    yf = combine_norm(x, yr, dest_tiles, norm_final, tc=tc)

    y_prompt = yf[:mp].reshape(bp, tp, d)
    y_sample = yf[mp:].reshape(bs, ts, d)
    return (y_prompt, y_sample, hg_p[None], hg_s[None], ssm_p[None], ssm_s[None], conv_p[None], conv_s[None])
```

```python
import functools

import jax
import jax.numpy as jnp
from jax import lax
from jax.experimental import pallas as pl
from jax.experimental.pallas import tpu as pltpu

EPS = 1e-5
CHUNK = 64
SUB = 16
HALF = SUB // 2
MB_GROUPS = 8
MB_HEADDIM = 64
MB_DSTATE = 128
MB_CONV = 4
TOP_K = 2
NEG_BIG = -1e30

V7X_VMEM_BYTES = 64 * 1024 * 1024
VMEM_LIMIT = V7X_VMEM_BYTES - 4 * 1024 * 1024
LANES = 128

F32 = jnp.float32
BF16 = jnp.bfloat16


def _cparams(sem):
    return pltpu.CompilerParams(dimension_semantics=sem, vmem_limit_bytes=VMEM_LIMIT)


def _sigmoid(x):
    return 1.0 / (1.0 + jnp.exp(-x))


def _silu(x):
    return x * _sigmoid(x)


def _split3(x):
    hi = x.astype(BF16)
    r = x - hi.astype(F32)
    mid = r.astype(BF16)
    lo = (r - mid.astype(F32)).astype(BF16)
    return hi, mid, lo


def _dot(a, b):
    return jnp.dot(a, b, preferred_element_type=F32)


def _dot_nt(a, b):
    return lax.dot_general(a, b, (((1,), (1,)), ((), ())), preferred_element_type=F32)


def _dot_tn(a, b):
    return lax.dot_general(a, b, (((0,), (0,)), ((), ())), preferred_element_type=F32)


def _rmsnorm_kernel(x_ref, g_ref, o_ref):
    x = x_ref[...]
    ms = jnp.mean(x * x, axis=-1, keepdims=True)
    o_ref[...] = (x * lax.rsqrt(ms + EPS) * g_ref[...]).astype(o_ref.dtype)


def rmsnorm(x, g, out_dtype, tr=256):
    m, d = x.shape
    return pl.pallas_call(
        _rmsnorm_kernel,
        grid=(m // tr,),
        in_specs=[pl.BlockSpec((tr, d), lambda i: (i, 0)), pl.BlockSpec((1, d), lambda i: (0, 0))],
        out_specs=pl.BlockSpec((tr, d), lambda i: (i, 0)),
        out_shape=jax.ShapeDtypeStruct((m, d), out_dtype),
        compiler_params=_cparams(("parallel",)),
        name="rmsnorm",
    )(x, g.reshape(1, d))


def _rmsnorm_router_kernel(x_ref, g_ref, wr_ref, o_ref, idx_ref, gate_ref, *, n_experts):
    x = x_ref[...]
    ms = jnp.mean(x * x, axis=-1, keepdims=True)
    hn = x * lax.rsqrt(ms + EPS) * g_ref[...]
    o_ref[...] = hn
    logits = jnp.dot(hn, wr_ref[...], preferred_element_type=F32, precision=lax.Precision.HIGHEST)
    lane = lax.broadcasted_iota(jnp.int32, logits.shape, 1).astype(F32)
    logits = jnp.where(lane < n_experts, logits, -jnp.inf)
    m1 = jnp.max(logits, axis=-1, keepdims=True)
    i1 = jnp.min(jnp.where(logits == m1, lane, float(LANES)), axis=-1, keepdims=True)
    rest = jnp.where(lane == i1, -jnp.inf, logits)
    m2 = jnp.max(rest, axis=-1, keepdims=True)
    i2 = jnp.min(jnp.where(rest == m2, lane, float(LANES)), axis=-1, keepdims=True)
    e2 = jnp.exp(m2 - m1)
    g1 = 1.0 / (1.0 + e2)
    g2 = e2 / (1.0 + e2)
    idx_ref[...] = jnp.where(lane == 0.0, i1, jnp.where(lane == 1.0, i2, 0.0)).astype(jnp.int32)
    gate_ref[...] = jnp.where(lane == 0.0, g1, jnp.where(lane == 1.0, g2, 0.0))


def rmsnorm_router(x, g, w_router, tr=256):
    m, d = x.shape
    n_experts = w_router.shape[1]
    wr = jnp.zeros((d, LANES), F32).at[:, :n_experts].set(w_router)
    hn, idx, gate = pl.pallas_call(
        functools.partial(_rmsnorm_router_kernel, n_experts=n_experts),
        grid=(m // tr,),
        in_specs=[pl.BlockSpec((tr, d), lambda i: (i, 0)), pl.BlockSpec((1, d), lambda i: (0, 0)),
                  pl.BlockSpec((d, LANES), lambda i: (0, 0))],
        out_specs=[pl.BlockSpec((tr, d), lambda i: (i, 0)), pl.BlockSpec((tr, LANES), lambda i: (i, 0)),
                   pl.BlockSpec((tr, LANES), lambda i: (i, 0))],
        out_shape=[jax.ShapeDtypeStruct((m, d), F32), jax.ShapeDtypeStruct((m, LANES), jnp.int32),
                   jax.ShapeDtypeStruct((m, LANES), F32)],
        compiler_params=_cparams(("parallel",)),
        name="rmsnorm_router",
    )(x, g.reshape(1, d), wr)
    return hn, idx[:, :TOP_K], gate[:, :TOP_K]


def _epilogue(mode, accs, res_ref, o_ref):
    if mode == "swiglu":
        o_ref[...] = (_silu(accs[0]) * accs[1]).astype(o_ref.dtype)
    elif mode == "res":
        o_ref[...] = res_ref[...] + accs[0]
    else:
        o_ref[...] = accs[0].astype(o_ref.dtype)


def _mm_kernel(*refs, nk, mode):
    n_w = 2 if mode == "swiglu" else 1
    a_ref = refs[0]
    w_refs = refs[1:1 + n_w]
    pos = 1 + n_w
    res_ref = None
    if mode == "res":
        res_ref = refs[pos]
        pos += 1
    o_ref = refs[pos]
    acc_refs = refs[pos + 1:]

    a = a_ref[...]
    parts = [_dot(a, w[...]) for w in w_refs]
    if nk == 1:
        _epilogue(mode, parts, res_ref, o_ref)
        return
    k = pl.program_id(2)

    @pl.when(k == 0)
    def _():
        for acc, p in zip(acc_refs, parts):
            acc[...] = p

    @pl.when(k > 0)
    def _():
        for acc, p in zip(acc_refs, parts):
            acc[...] += p

    @pl.when(k == nk - 1)
    def _():
        _epilogue(mode, [acc[...] for acc in acc_refs], res_ref, o_ref)


def matmul(a, ws, *, mode="plain", res=None, out_dtype=F32, tm, tn, tk, name):
    m, kdim = a.shape
    n_out = ws[0].shape[1]
    nk = kdim // tk
    grid = (m // tm, n_out // tn, nk)
    in_specs = [pl.BlockSpec((tm, tk), lambda i, j, k: (i, k))]
    in_specs += [pl.BlockSpec((tk, tn), lambda i, j, k: (k, j)) for _ in ws]
    args = [a, *ws]
    if mode == "res":
        in_specs.append(pl.BlockSpec((tm, tn), lambda i, j, k: (i, j)))
        args.append(res)
    scratch = [pltpu.VMEM((tm, tn), F32) for _ in ws] if nk > 1 else []
    return pl.pallas_call(
        functools.partial(_mm_kernel, nk=nk, mode=mode),
        grid=grid,
        in_specs=in_specs,
        out_specs=pl.BlockSpec((tm, tn), lambda i, j, k: (i, j)),
        out_shape=jax.ShapeDtypeStruct((m, n_out), out_dtype),
        scratch_shapes=scratch,
        compiler_params=_cparams(("parallel", "parallel", "arbitrary")),
        name=name,
    )(*args)


def _mm_ws_kernel(*refs, mode):
    n_w = 2 if mode == "swiglu" else 1
    a_ref = refs[0]
    w_refs = refs[1:1 + n_w]
    o_ref = refs[1 + n_w]
    ws_refs = refs[2 + n_w:]

    @pl.when(pl.program_id(1) == 0)
    def _():
        for w, ws in zip(w_refs, ws_refs):
            ws[...] = w[...].astype(BF16)

    a = a_ref[...]
    _epilogue(mode, [_dot(a, ws[...]) for ws in ws_refs], None, o_ref)


def matmul_ws(a, ws, *, mode="plain", out_dtype=F32, n_out=None, col_off=0, tm, tn, name):
    m, kdim = a.shape
    n_out = ws[0].shape[1] if n_out is None else n_out
    grid = (n_out // tn, m // tm)
    in_specs = [pl.BlockSpec((tm, kdim), lambda j, i: (i, 0))]
    in_specs += [pl.BlockSpec((kdim, tn), lambda j, i: (0, j + col_off)) for _ in ws]
    return pl.pallas_call(
        functools.partial(_mm_ws_kernel, mode=mode),
        grid=grid,
        in_specs=in_specs,
        out_specs=pl.BlockSpec((tm, tn), lambda j, i: (i, j)),
        out_shape=jax.ShapeDtypeStruct((m, n_out), out_dtype),
        scratch_shapes=[pltpu.VMEM((kdim, tn), BF16) for _ in ws],
        compiler_params=_cparams(("parallel", "arbitrary")),
        name=name,
    )(a, *ws)


def _hgrn_chunk(q_raw, fz, v, g_raw, lb, ng, st, c):
    q = _silu(q_raw)
    f = lb + (1.0 - lb) * _sigmoid(fz)
    lf = jnp.log2(f)
    k = 1.0 - f
    row = lax.broadcasted_iota(jnp.int32, (c, c), 0)
    col = lax.broadcasted_iota(jnp.int32, (c, c), 1)
    tri = (row >= col).astype(BF16)
    hi, mid, lo = _split3(lf)
    b = _dot(tri, hi) + _dot(tri, mid) + _dot(tri, lo)
    b_end = b[c - 1:c, :]

    o = _dot_nt((q * jnp.exp2(b)).astype(BF16), st.astype(BF16))
    r8 = lax.broadcasted_iota(jnp.int32, (HALF, 1), 0)

    def pair_term(qq, bb, bs, ks, vs, mask_from):
        d = bb - bs
        if mask_from is not None:
            d = jnp.where(r8 >= mask_from, d, NEG_BIG)
        return jnp.sum(qq * jnp.exp2(d) * ks, axis=-1, keepdims=True) * vs

    outs = []
    for i in range(c // SUB):
        lo_r, md_r, hi_r = i * SUB, i * SUB + HALF, (i + 1) * SUB
        top, bot = slice(lo_r, md_r), slice(md_r, hi_r)
        ot, ob = o[top], o[bot]
        for s in range(HALF):
            bs, ks, vs = b[lo_r + s:lo_r + s + 1], k[lo_r + s:lo_r + s + 1], v[lo_r + s:lo_r + s + 1]
            ot = ot + pair_term(q[top], b[top], bs, ks, vs, s)
            ob = ob + pair_term(q[bot], b[bot], bs, ks, vs, None)
        for s in range(HALF):
            bs, ks, vs = b[md_r + s:md_r + s + 1], k[md_r + s:md_r + s + 1], v[md_r + s:md_r + s + 1]
            ob = ob + pair_term(q[bot], b[bot], bs, ks, vs, s)
        oi = jnp.concatenate([ot, ob], axis=0)
        if i > 0:
            anchor = b[lo_r - 1:lo_r, :]
            qt = (q[lo_r:hi_r] * jnp.exp2(b[lo_r:hi_r] - anchor)).astype(BF16)
            kt = (k[:lo_r] * jnp.exp2(anchor - b[:lo_r])).astype(BF16)
            sc = _dot_nt(qt, kt)
            oi = oi + _dot(sc.astype(BF16), v[:lo_r].astype(BF16))
        outs.append(oi)
    o = outs[0] if len(outs) == 1 else jnp.concatenate(outs, axis=0)

    kd = (k * jnp.exp2(b_end - b)).astype(BF16)
    st_new = st * jnp.exp2(b_end) + _dot_tn(v.astype(BF16), kd)

    o = o * lax.rsqrt(jnp.mean(o * o, axis=-1, keepdims=True) + EPS)
    return o * ng * _silu(g_raw), st_new


def _hgrn_prompt_kernel(q_ref, f_ref, v_ref, g_ref, lb_ref, ng_ref, o_ref, s_ref, st_ref, *, hb, c, dk):
    t = q_ref.shape[0]
    st_ref[...] = jnp.zeros_like(st_ref)

    def body(ci, carry):
        r0 = pl.multiple_of(ci * c, c)
        for h in range(hb):
            ls = slice(h * dk, (h + 1) * dk)
            o, st_new = _hgrn_chunk(q_ref[pl.ds(r0, c), ls], f_ref[pl.ds(r0, c), ls], v_ref[pl.ds(r0, c), ls],
                                    g_ref[pl.ds(r0, c), ls], lb_ref[:, ls], ng_ref[:, ls], st_ref[h], c)
            st_ref[h] = st_new
            o_ref[pl.ds(r0, c), ls] = o.astype(o_ref.dtype)
        return carry

    lax.fori_loop(0, t // c, body, 0, unroll=4)
    for h in range(hb):
        s_ref[0, h] = st_ref[h].T


def _hgrn_sample_kernel(q_ref, f_ref, v_ref, g_ref, lb_ref, ng_ref, s0_ref, o_ref, s_ref, *, hb, c, dk):
    nb = s0_ref.shape[0]

    def body(bi, carry):
        r0 = pl.multiple_of(bi * c, c)
        for h in range(hb):
            ls = slice(h * dk, (h + 1) * dk)
            o, st_new = _hgrn_chunk(q_ref[pl.ds(r0, c), ls], f_ref[pl.ds(r0, c), ls], v_ref[pl.ds(r0, c), ls],
                                    g_ref[pl.ds(r0, c), ls], lb_ref[:, ls], ng_ref[:, ls], s0_ref[bi, h].T, c)
            s_ref[bi, h] = st_new.T
            o_ref[pl.ds(r0, c), ls] = o.astype(o_ref.dtype)
        return carry

    lax.fori_loop(0, nb, body, 0, unroll=2)


def hgrn_scan(proj, lb, ng, s0, *, row0, nbatch, t, heads, dk, hb):
    w = hb * dk
    nhb = heads // hb
    lb2 = lb.reshape(1, heads * dk)
    ng2 = ng.reshape(1, heads * dk)
    if s0 is None:
        c = min(CHUNK, t)
        rb0 = row0 // t
        kern = functools.partial(_hgrn_prompt_kernel, hb=hb, c=c, dk=dk)
        col = lambda part: pl.BlockSpec((t, w), lambda b, j, part=part: (rb0 + b, part * nhb + j))
        vec = pl.BlockSpec((1, w), lambda b, j: (0, j))
        return pl.pallas_call(
            kern,
            grid=(nbatch, nhb),
            in_specs=[col(0), col(1), col(2), col(3), vec, vec],
            out_specs=[pl.BlockSpec((t, w), lambda b, j: (b, j)),
                       pl.BlockSpec((1, hb, dk, dk), lambda b, j: (b, j, 0, 0))],
            out_shape=[jax.ShapeDtypeStruct((nbatch * t, heads * dk), BF16),
                       jax.ShapeDtypeStruct((nbatch, heads, dk, dk), F32)],
            scratch_shapes=[pltpu.VMEM((hb, dk, dk), F32)],
            compiler_params=_cparams(("parallel", "parallel")),
            name="hgrn_prompt",
        )(proj, proj, proj, proj, lb2, ng2)
    rows = nbatch * t
    rb0 = row0 // rows
    kern = functools.partial(_hgrn_sample_kernel, hb=hb, c=t, dk=dk)
    col = lambda part: pl.BlockSpec((rows, w), lambda j, part=part: (rb0, part * nhb + j))
    vec = pl.BlockSpec((1, w), lambda j: (0, j))
    return pl.pallas_call(
        kern,
        grid=(nhb,),
        in_specs=[col(0), col(1), col(2), col(3), vec, vec,
                  pl.BlockSpec((nbatch, hb, dk, dk), lambda j: (0, j, 0, 0))],
        out_specs=[pl.BlockSpec((rows, w), lambda j: (0, j)),
                   pl.BlockSpec((nbatch, hb, dk, dk), lambda j: (0, j, 0, 0))],
        out_shape=[jax.ShapeDtypeStruct((rows, heads * dk), BF16),
                   jax.ShapeDtypeStruct((nbatch, heads, dk, dk), F32)],
        compiler_params=_cparams(("parallel",)),
        name="hgrn_sample",
    )(proj, proj, proj, proj, lb2, ng2, s0)


def _conv_silu(x, pad_ref, w_ref, bias_ref, c):
    pad_ref[pl.ds(8, c), :] = x
    acc = bias_ref[...] + w_ref[MB_CONV - 1:MB_CONV, :] * x
    for j in range(MB_CONV - 1):
        acc = acc + w_ref[j:j + 1, :] * pad_ref[pl.ds(5 + j, c), :]
    pad_ref[pl.ds(5, 3), :] = pad_ref[pl.ds(5 + c, 3), :]
    return _silu(acc)


def _ssd_kernel(z_ref, x_ref, b_ref, c_ref, dt_ref, *rest, c, heads, has_state):
    if has_state:
        conv0x_ref, conv0b_ref, conv0c_ref, s0_ref = rest[:4]
        rest = rest[4:]
    (wx_ref, wb_ref, wc_ref, bx_ref, bb_ref, bc_ref, dtb_ref, alog_ref, dsk_ref, ng_ref, ep_ref, ec_ref,
     y_ref, s_ref, st_ref, padx_ref, padb_ref, padc_ref) = rest
    p = MB_HEADDIM
    n = MB_DSTATE
    hpg = heads // MB_GROUPS
    gw = hpg * p
    ci = pl.program_id(1)
    nci = pl.num_programs(1)

    @pl.when(ci == 0)
    def _():
        if has_state:
            padx_ref[pl.ds(5, 3), :] = conv0x_ref[0]
            padb_ref[pl.ds(5, 3), :] = conv0b_ref[0]
            padc_ref[pl.ds(5, 3), :] = conv0c_ref[0]
            st_ref[...] = s0_ref[0].T
        else:
            padx_ref[pl.ds(5, 3), :] = jnp.zeros((3, padx_ref.shape[1]), F32)
            padb_ref[pl.ds(5, 3), :] = jnp.zeros((3, padb_ref.shape[1]), F32)
            padc_ref[pl.ds(5, 3), :] = jnp.zeros((3, padc_ref.shape[1]), F32)
            st_ref[...] = jnp.zeros_like(st_ref)

    bm = _conv_silu(b_ref[...], padb_ref, wb_ref, bb_ref, c)
    cm = _conv_silu(c_ref[...], padc_ref, wc_ref, bc_ref, c)

    dtr = dt_ref[...] + dtb_ref[...]
    dt = jnp.maximum(dtr, 0.0) + jnp.log1p(jnp.exp(-jnp.abs(dtr)))
    la = dt * (-jnp.exp(alog_ref[...]))
    row = lax.broadcasted_iota(jnp.int32, (c, c), 0)
    col = lax.broadcasted_iota(jnp.int32, (c, c), 1)
    tri = (row >= col).astype(BF16)
    hi, mid, lo = _split3(la)
    cs = _dot(tri, hi) + _dot(tri, mid) + _dot(tri, lo)
    cs3 = _split3(cs)
    dt3 = _split3(dt)

    mw = hpg * c
    t_m = lax.broadcasted_iota(jnp.int32, (c, mw), 0)
    s_m = lax.broadcasted_iota(jnp.int32, (c, mw), 1) % c
    nbk = LANES // c
    bd_rows = lax.broadcasted_iota(jnp.int32, (LANES, nbk * p), 0) // c
    bd_cols = lax.broadcasted_iota(jnp.int32, (LANES, nbk * p), 1) // p
    bd_mask = bd_rows == bd_cols

    for g in range(MB_GROUPS):
        xl = slice(g * gw, (g + 1) * gw)
        nl = slice(g * n, (g + 1) * n)
        xs = _conv_silu(x_ref[:, xl], padx_ref.at[:, xl], wx_ref.at[:, xl], bx_ref.at[:, xl], c)
        bg = bm[:, nl]
        cg = cm[:, nl]
        ep = ep_ref[:, xl]
        cse = _dot(cs3[0], ep) + _dot(cs3[1], ep) + _dot(cs3[2], ep)
        dte = _dot(dt3[0], ep) + _dot(dt3[1], ep) + _dot(dt3[2], ep)
        cs_end = cse[c - 1:c, :]
        xdt = xs * dte
        xw = xdt * jnp.exp(cs_end - cse)

        st_g = st_ref[:, xl]
        y = _dot(cg.astype(BF16), st_g.astype(BF16)) * jnp.exp(cse)
        st_ref[:, xl] = st_g * jnp.exp(cs_end) + _dot_tn(bg.astype(BF16), xw.astype(BF16))

        ec = ec_ref[:, g * mw:(g + 1) * mw]
        csm = _dot(cs3[0], ec) + _dot(cs3[1], ec) + _dot(cs3[2], ec)
        cs_s = jnp.sum(jnp.where(t_m == s_m, csm, 0.0), axis=0, keepdims=True)
        lw = jnp.exp(jnp.where(t_m >= s_m, csm - cs_s, NEG_BIG))
        b_t = jnp.concatenate([bg] * hpg, axis=0).astype(BF16)
        mm = (_dot_nt(cg.astype(BF16), b_t) * lw).astype(BF16)
        xdt_b = xdt.astype(BF16)
        parts = []
        for blk in range(mw // LANES):
            xb = xdt_b[:, blk * nbk * p:(blk + 1) * nbk * p]
            bd = jnp.where(bd_mask, jnp.concatenate([xb] * nbk, axis=0), jnp.zeros((), BF16))
            parts.append(_dot(mm[:, blk * LANES:(blk + 1) * LANES], bd))
        y = y + jnp.concatenate(parts, axis=1) + dsk_ref[:, xl] * xs

        y = y * _silu(z_ref[:, xl])
        y = y * lax.rsqrt(jnp.mean(y * y, axis=-1, keepdims=True) + EPS)
        y_ref[:, xl] = (y * ng_ref[:, xl]).astype(y_ref.dtype)

    @pl.when(ci == nci - 1)
    def _():
        s_ref[0] = st_ref[...].T


def ssd_scan(zx, dt_raw, conv0, s0, conv_w, conv_b, dt_bias, a_log, d_skip, norm_g, *, row0, nbatch, t, heads):
    p, n = MB_HEADDIM, MB_DSTATE
    di = heads * p
    gn = MB_GROUPS * n
    c = min(CHUNK, t)
    nch = t // c
    rb0 = row0 // c
    has_state = s0 is not None
    head_of_x = jnp.arange(di) // p
    ep = (jnp.arange(heads)[:, None] == head_of_x[None, :]).astype(BF16)
    head_of_m = jnp.arange(heads * c) // c
    ec = (jnp.arange(heads)[:, None] == head_of_m[None, :]).astype(BF16)
    dsk = jnp.repeat(d_skip.astype(F32), p).reshape(1, di)

    row = lambda w, blk: pl.BlockSpec((c, w), lambda b, k, blk=blk: (rb0 + b * nch + k, blk))
    full = lambda a: pl.BlockSpec(a.shape, lambda b, k: (0,) * a.ndim)
    in_specs = [row(di, 0), row(di, 1), row(gn, 2 * di // gn), row(gn, 2 * di // gn + 1), row(heads, 0)]
    args = [zx, zx, zx, zx, dt_raw]
    if has_state:
        in_specs += [pl.BlockSpec((1, MB_CONV - 1, di), lambda b, k: (b, 0, 0)),
                     pl.BlockSpec((1, MB_CONV - 1, gn), lambda b, k: (b, 0, 0)),
                     pl.BlockSpec((1, MB_CONV - 1, gn), lambda b, k: (b, 0, 0)),
                     pl.BlockSpec((1, di, n), lambda b, k: (b, 0, 0))]
        args += [conv0[:, :, :di], conv0[:, :, di:di + gn], conv0[:, :, di + gn:], s0.reshape(nbatch, di, n)]
    consts = [conv_w[:, :di], conv_w[:, di:di + gn], conv_w[:, di + gn:],
              conv_b[:di].reshape(1, di), conv_b[di:di + gn].reshape(1, gn), conv_b[di + gn:].reshape(1, gn),
              dt_bias.reshape(1, heads), a_log.reshape(1, heads), dsk, norm_g.reshape(1, di), ep, ec]
    in_specs += [full(a) for a in consts]
    args += consts
    y, s_fin = pl.pallas_call(
        functools.partial(_ssd_kernel, c=c, heads=heads, has_state=has_state),
        grid=(nbatch, nch),
        in_specs=in_specs,
        out_specs=[pl.BlockSpec((c, di), lambda b, k: (b * nch + k, 0)),
                   pl.BlockSpec((1, di, n), lambda b, k: (b, 0, 0))],
        out_shape=[jax.ShapeDtypeStruct((nbatch * t, di), BF16), jax.ShapeDtypeStruct((nbatch, di, n), F32)],
        scratch_shapes=[pltpu.VMEM((n, di), F32), pltpu.VMEM((c + 8, di), F32),
                        pltpu.VMEM((c + 8, gn), F32), pltpu.VMEM((c + 8, gn), F32)],
        compiler_params=_cparams(("parallel", "arbitrary")),
        name="ssd_state" if has_state else "ssd_prompt",
    )(*args)
    return y, s_fin.reshape(nbatch, heads, p, n)


def _gather_rows_kernel(idx_ref, src_ref, o_ref, buf_ref, sem, *, tg):
    i = pl.program_id(0)

    def copy(r):
        return pltpu.make_async_copy(src_ref.at[pl.ds(idx_ref[i * tg + r], 1)], buf_ref.at[pl.ds(r, 1)], sem)

    def start(r, carry):
        copy(r).start()
        return carry

    def wait(r, carry):
        copy(r).wait()
        return carry

    lax.fori_loop(0, tg, start, 0)
    lax.fori_loop(0, tg, wait, 0)
    o_ref[...] = buf_ref[...].astype(o_ref.dtype)


def gather_rows(src, idx, out_dtype, tg=256):
    rows = idx.shape[0]
    d = src.shape[1]
    return pl.pallas_call(
        functools.partial(_gather_rows_kernel, tg=tg),
        grid_spec=pltpu.PrefetchScalarGridSpec(
            num_scalar_prefetch=1,
            grid=(rows // tg,),
            in_specs=[pl.BlockSpec(memory_space=pl.ANY)],
            out_specs=pl.BlockSpec((tg, d), lambda i, idx: (i, 0)),
            scratch_shapes=[pltpu.VMEM((tg, d), src.dtype), pltpu.SemaphoreType.DMA(())],
        ),
        out_shape=jax.ShapeDtypeStruct((rows, d), out_dtype),
        compiler_params=_cparams(("arbitrary",)),
        name="moe_gather",
    )(idx, src)


def _moe_ffn1_kernel(first_ref, ntile_ref, w1_ref, w3_ref, x_hbm, o_hbm, w1s, w3s, xbuf, obuf, xsem, osem,
                     *, tmg, tn, n_experts):
    j = pl.program_id(0)
    e = pl.program_id(1)
    t0 = first_ref[e]
    nt = ntile_ref[e]
    col0 = pl.multiple_of(j * tn, tn)

    def x_copy(t, slot):
        r0 = pl.multiple_of((t0 + t) * tmg, tmg)
        return pltpu.make_async_copy(x_hbm.at[pl.ds(r0, tmg)], xbuf.at[slot], xsem.at[slot])

    def o_copy(t, slot):
        r0 = pl.multiple_of((t0 + t) * tmg, tmg)
        return pltpu.make_async_copy(obuf.at[slot], o_hbm.at[pl.ds(r0, tmg), pl.ds(col0, tn)], osem.at[slot])

    is_expert = e < n_experts

    @pl.when(jnp.logical_and(is_expert, nt > 0))
    def _():
        x_copy(0, 0).start()
        w1s[...] = w1_ref[0].astype(BF16)
        w3s[...] = w3_ref[0].astype(BF16)

    def body(t, carry):
        slot = t % 2

        @pl.when(t >= 2)
        def _():
            o_copy(t - 2, slot).wait()

        @pl.when(is_expert)
        def _():
            x_copy(t, slot).wait()

            @pl.when(t + 1 < nt)
            def _():
                x_copy(t + 1, 1 - slot).start()

            a = xbuf[slot]
            obuf[slot] = (_silu(_dot(a, w1s[...])) * _dot(a, w3s[...])).astype(obuf.dtype)

        @pl.when(jnp.logical_not(is_expert))
        def _():
            obuf[slot] = jnp.zeros(obuf.shape[1:], obuf.dtype)

        o_copy(t, slot).start()
        return carry

    lax.fori_loop(0, nt, body, 0)

    @pl.when(nt >= 2)
    def _():
        o_copy(nt - 2, nt % 2).wait()

    @pl.when(nt >= 1)
    def _():
        o_copy(nt - 1, (nt - 1) % 2).wait()


def moe_ffn1(xs, w1, w3, tile_first, tile_count, *, tmg, tn):
    rp, d = xs.shape
    n_experts, _, dff = w1.shape

    def w_map(j, e, first, count):
        return (jnp.minimum(e, n_experts - 1), 0, j)

    return pl.pallas_call(
        functools.partial(_moe_ffn1_kernel, tmg=tmg, tn=tn, n_experts=n_experts),
        grid_spec=pltpu.PrefetchScalarGridSpec(
            num_scalar_prefetch=2,
            grid=(dff // tn, n_experts + 1),
            in_specs=[pl.BlockSpec((1, d, tn), w_map), pl.BlockSpec((1, d, tn), w_map),
                      pl.BlockSpec(memory_space=pl.ANY)],
            out_specs=pl.BlockSpec(memory_space=pl.ANY),
            scratch_shapes=[pltpu.VMEM((d, tn), BF16), pltpu.VMEM((d, tn), BF16),
                            pltpu.VMEM((2, tmg, d), BF16), pltpu.VMEM((2, tmg, tn), BF16),
                            pltpu.SemaphoreType.DMA((2,)), pltpu.SemaphoreType.DMA((2,))],
        ),
        out_shape=jax.ShapeDtypeStruct((rp, dff), BF16),
        compiler_params=_cparams(("arbitrary", "arbitrary")),
        name="moe_ffn1",
    )(tile_first, tile_count, w1, w3, xs)


def _moe_ffn2_kernel(te_ref, nu_ref, h_ref, w2_ref, gate_ref, o_ref):
    used = pl.program_id(1) < nu_ref[0]

    @pl.when(used)
    def _():
        o_ref[...] = _dot(h_ref[...], w2_ref[0]) * gate_ref[...]

    @pl.when(jnp.logical_not(used))
    def _():
        o_ref[...] = jnp.zeros_like(o_ref)


def moe_ffn2(h, w2, row_gate, tile_expert, n_used, *, tmg, tn):
    rp, dff = h.shape
    d = w2.shape[2]
    nt = rp // tmg

    def a_map(j, i, te, nu):
        return (jnp.minimum(i, nu[0] - 1), 0)

    def w_map(j, i, te, nu):
        return (te[i], 0, j)

    def o_map(j, i, te, nu):
        return (i, j)

    return pl.pallas_call(
        _moe_ffn2_kernel,
        grid_spec=pltpu.PrefetchScalarGridSpec(
            num_scalar_prefetch=2,
            grid=(d // tn, nt),
            in_specs=[pl.BlockSpec((tmg, dff), a_map), pl.BlockSpec((1, dff, tn), w_map),
                      pl.BlockSpec((tmg, 1), a_map)],
            out_specs=pl.BlockSpec((tmg, tn), o_map),
        ),
        out_shape=jax.ShapeDtypeStruct((rp, d), F32),
        compiler_params=_cparams(("parallel", "arbitrary")),
        name="moe_ffn2",
    )(tile_expert, n_used, h, w2, row_gate)


def _combine_norm_kernel(dest_ref, x_ref, y_ref, g_ref, o_ref, buf_ref, sem, *, tc):
    i = pl.program_id(0)

    def copy(r):
        row = dest_ref[i * TOP_K * tc + r]
        return pltpu.make_async_copy(y_ref.at[pl.ds(row, 1)], buf_ref.at[pl.ds(r, 1)], sem)

    def start(r, carry):
        copy(r).start()
        return carry

    def wait(r, carry):
        copy(r).wait()
        return carry

    lax.fori_loop(0, TOP_K * tc, start, 0)
    lax.fori_loop(0, TOP_K * tc, wait, 0)
    x = x_ref[...]
    for kk in range(TOP_K):
        x = x + buf_ref[pl.ds(kk * tc, tc), :]
    ms = jnp.mean(x * x, axis=-1, keepdims=True)
    o_ref[...] = x * lax.rsqrt(ms + EPS) * g_ref[...]


def combine_norm(x, y_rows, dest, g, tc=128):
    m, d = x.shape
    return pl.pallas_call(
        functools.partial(_combine_norm_kernel, tc=tc),
        grid_spec=pltpu.PrefetchScalarGridSpec(
            num_scalar_prefetch=1,
            grid=(m // tc,),
            in_specs=[pl.BlockSpec((tc, d), lambda i, dest: (i, 0)), pl.BlockSpec(memory_space=pl.ANY),
                      pl.BlockSpec((1, d), lambda i, dest: (0, 0))],
            out_specs=pl.BlockSpec((tc, d), lambda i, dest: (i, 0)),
            scratch_shapes=[pltpu.VMEM((TOP_K * tc, d), F32), pltpu.SemaphoreType.DMA(())],
        ),
        out_shape=jax.ShapeDtypeStruct((m, d), F32),
        compiler_params=_cparams(("arbitrary",)),
        name="moe_combine_norm",
    )(dest, x, y_rows, g.reshape(1, d))


def moe_dispatch(top_i, gates, n_experts, tmg):
    m = top_i.shape[0]
    na = m * TOP_K
    nt = na // tmg + n_experts
    flat_e = top_i.reshape(na).astype(jnp.int32)
    order = jnp.argsort(flat_e, stable=True).astype(jnp.int32)
    counts = jnp.zeros((n_experts,), jnp.int32).at[flat_e].add(1)
    padded = (counts + tmg - 1) // tmg * tmg
    pad_end = jnp.cumsum(padded)
    pad_start = pad_end - padded
    cnt_start = jnp.cumsum(counts) - counts
    sorted_e = flat_e[order]
    dest_sorted = pad_start[sorted_e] + jnp.arange(na, dtype=jnp.int32) - cnt_start[sorted_e]
    row_token = jnp.zeros((nt * tmg,), jnp.int32).at[dest_sorted].set(order // TOP_K)
    row_gate = jnp.zeros((nt * tmg,), F32).at[dest_sorted].set(gates.reshape(na)[order])
    dest = jnp.zeros((na,), jnp.int32).at[order].set(dest_sorted)
    n_used = (pad_end[-1] // tmg).astype(jnp.int32).reshape(1)
    tile_start = jnp.arange(nt, dtype=jnp.int32) * tmg
    tile_expert = jnp.searchsorted(pad_end, tile_start, side="right").astype(jnp.int32)
    last_expert = tile_expert[jnp.maximum(n_used[0] - 1, 0)]
    tile_expert = jnp.where(tile_start < pad_end[-1], tile_expert, last_expert)
    tile_first = jnp.concatenate([pad_start // tmg, n_used]).astype(jnp.int32)
    tile_count = jnp.concatenate([padded // tmg, nt - n_used]).astype(jnp.int32)
    return row_token, row_gate.reshape(-1, 1), dest, tile_expert, n_used, tile_first, tile_count


def kernel(x_prompt, x_sample, state_hgrn, state_ssm, state_conv, norm_mix, norm_ffn, norm_final, hg_w_in, hg_lb_logits, hg_norm, hg_w_out, mb_w_in, mb_conv_w, mb_conv_b, mb_dt_bias, mb_a_log, mb_d, mb_norm, mb_w_out, ffn_w1, ffn_w3, ffn_w2, moe_router, moe_w1, moe_w3, moe_w2):
    bp, tp, d = x_prompt.shape
    bs, ts, _ = x_sample.shape
    mp, ms = bp * tp, bs * ts
    m = mp + ms
    hg_heads, hg_dk = state_hgrn.shape[2], state_hgrn.shape[3]
    mb_heads = state_ssm.shape[2]
    di = mb_heads * MB_HEADDIM
    conv_dim = state_conv.shape[3]
    n_experts = moe_router.shape[2]
    tm = m // 8
    tm_ws = m // 16

    x = jnp.concatenate([x_prompt.reshape(mp, d), x_sample.reshape(ms, d)], axis=0)
    bf = lambda w: w.astype(BF16)

    lbs = jnp.cumsum(jax.nn.softmax(hg_lb_logits.astype(F32), axis=0), axis=0)
    hn = rmsnorm(x, norm_mix[0], BF16)
    proj = matmul_ws(hn, [hg_w_in[0]], tm=tm_ws, tn=1024, name="hg_in")
    scan = functools.partial(hgrn_scan, proj, lbs[0], hg_norm[0], heads=hg_heads, dk=hg_dk, hb=2)
    o_p, hg_p = scan(None, row0=0, nbatch=bp, t=tp)
    o_s, hg_s = scan(state_hgrn[0], row0=mp, nbatch=bs, t=ts)
    o = jnp.concatenate([o_p, o_s], axis=0)
    x = matmul(o, [bf(hg_w_out[0])], mode="res", res=x, tm=tm, tn=1024, tk=2048, name="hg_out")
    hn = rmsnorm(x, norm_ffn[0], BF16)
    h = matmul_ws(hn, [ffn_w1[0], ffn_w3[0]], mode="swiglu", out_dtype=BF16, tm=tm_ws, tn=512, name="ffn_in")
    x = matmul(h, [bf(ffn_w2[0])], mode="res", res=x, tm=tm, tn=1024, tk=2048, name="ffn_out")

    hn = rmsnorm(x, norm_mix[1], BF16)
    zx = matmul_ws(hn, [mb_w_in[0]], n_out=di + conv_dim, tm=tm_ws, tn=1024, name="mb_in")
    dt_raw = matmul_ws(hn, [mb_w_in[0]], n_out=mb_heads, col_off=(di + conv_dim) // mb_heads, tm=tm,
                       tn=mb_heads, name="mb_in_dt")
    ssd = functools.partial(ssd_scan, zx, dt_raw, conv_w=mb_conv_w[0], conv_b=mb_conv_b[0], dt_bias=mb_dt_bias[0],
                            a_log=mb_a_log[0], d_skip=mb_d[0], norm_g=mb_norm[0], heads=mb_heads)
    y_p, ssm_p = ssd(None, None, row0=0, nbatch=bp, t=tp)
    y_s, ssm_s = ssd(state_conv[0], state_ssm[0], row0=mp, nbatch=bs, t=ts)
    keep = MB_CONV - 1
    conv_p = zx[:mp].reshape(bp, tp, di + conv_dim)[:, tp - keep:, di:]
    conv_s = zx[mp:].reshape(bs, ts, di + conv_dim)[:, ts - keep:, di:]
    y = jnp.concatenate([y_p, y_s], axis=0)
    x = matmul(y, [bf(mb_w_out[0])], mode="res", res=x, tm=tm, tn=1024, tk=2048, name="mb_out")

    tmg = 256
    hn32, top_i, gates = rmsnorm_router(x, norm_ffn[1], moe_router[0])
    row_token, row_gate, dest, tile_expert, n_used, tile_first, tile_count = moe_dispatch(top_i, gates, n_experts, tmg)
    xs = gather_rows(hn32, row_token, BF16)
    hmid = moe_ffn1(xs, moe_w1[0], moe_w3[0], tile_first, tile_count, tmg=tmg, tn=512)
    yr = moe_ffn2(hmid, bf(moe_w2[0]), row_gate, tile_expert, n_used, tmg=tmg, tn=512)
    tc = 128
    dest_tiles = dest.reshape(m // tc, tc, TOP_K).transpose(0, 2, 1).reshape(-1)
    yf = combine_norm(x, yr, dest_tiles, norm_final, tc=tc)

    y_prompt = yf[:mp].reshape(bp, tp, d)
    y_sample = yf[mp:].reshape(bs, ts, d)
    return (y_prompt, y_sample, hg_p[None], hg_s[None], ssm_p[None], ssm_s[None], conv_p[None], conv_s[None])
```

```python
import functools

import jax
import jax.numpy as jnp
from jax import lax
from jax.experimental import pallas as pl
from jax.experimental.pallas import tpu as pltpu

EPS = 1e-5
CHUNK = 64
SUB = 16
HALF = SUB // 2
MB_GROUPS = 8
MB_HEADDIM = 64
MB_DSTATE = 128
MB_CONV = 4
TOP_K = 2
NEG_BIG = -1e30

V7X_VMEM_BYTES = 64 * 1024 * 1024
VMEM_LIMIT = V7X_VMEM_BYTES - 4 * 1024 * 1024
LANES = 128

F32 = jnp.float32
BF16 = jnp.bfloat16


def _cparams(sem):
    return pltpu.CompilerParams(dimension_semantics=sem, vmem_limit_bytes=VMEM_LIMIT)


def _sigmoid(x):
    return 1.0 / (1.0 + jnp.exp(-x))


def _silu(x):
    return x * _sigmoid(x)


def _split3(x):
    hi = x.astype(BF16)
    r = x - hi.astype(F32)
    mid = r.astype(BF16)
    lo = (r - mid.astype(F32)).astype(BF16)
    return hi, mid, lo


def _dot(a, b):
    return jnp.dot(a, b, preferred_element_type=F32)


def _dot_nt(a, b):
    return lax.dot_general(a, b, (((1,), (1,)), ((), ())), preferred_element_type=F32)


def _dot_tn(a, b):
    return lax.dot_general(a, b, (((0,), (0,)), ((), ())), preferred_element_type=F32)


def _rmsnorm_kernel(x_ref, g_ref, o_ref):
    x = x_ref[...]
    ms = jnp.mean(x * x, axis=-1, keepdims=True)
    o_ref[...] = (x * lax.rsqrt(ms + EPS) * g_ref[...]).astype(o_ref.dtype)


def rmsnorm(x, g, out_dtype, tr=256):
    m, d = x.shape
    return pl.pallas_call(
        _rmsnorm_kernel,
        grid=(m // tr,),
        in_specs=[pl.BlockSpec((tr, d), lambda i: (i, 0)), pl.BlockSpec((1, d), lambda i: (0, 0))],
        out_specs=pl.BlockSpec((tr, d), lambda i: (i, 0)),
        out_shape=jax.ShapeDtypeStruct((m, d), out_dtype),
        compiler_params=_cparams(("parallel",)),
        name="rmsnorm",
    )(x, g.reshape(1, d))


def _rmsnorm_router_kernel(x_ref, g_ref, wr_ref, o_ref, idx_ref, gate_ref, *, n_experts):
    x = x_ref[...]
    ms = jnp.mean(x * x, axis=-1, keepdims=True)
    hn = x * lax.rsqrt(ms + EPS) * g_ref[...]
    o_ref[...] = hn
    logits = jnp.dot(hn, wr_ref[...], preferred_element_type=F32, precision=lax.Precision.HIGHEST)
    lane = lax.broadcasted_iota(jnp.int32, logits.shape, 1).astype(F32)
    logits = jnp.where(lane < n_experts, logits, -jnp.inf)
    m1 = jnp.max(logits, axis=-1, keepdims=True)
    i1 = jnp.min(jnp.where(logits == m1, lane, float(LANES)), axis=-1, keepdims=True)
    rest = jnp.where(lane == i1, -jnp.inf, logits)
    m2 = jnp.max(rest, axis=-1, keepdims=True)
    i2 = jnp.min(jnp.where(rest == m2, lane, float(LANES)), axis=-1, keepdims=True)
    e2 = jnp.exp(m2 - m1)
    g1 = 1.0 / (1.0 + e2)
    g2 = e2 / (1.0 + e2)
    idx_ref[...] = jnp.where(lane == 0.0, i1, jnp.where(lane == 1.0, i2, 0.0)).astype(jnp.int32)
    gate_ref[...] = jnp.where(lane == 0.0, g1, jnp.where(lane == 1.0, g2, 0.0))


def rmsnorm_router(x, g, w_router, tr=256):
    m, d = x.shape
    n_experts = w_router.shape[1]
    wr = jnp.zeros((d, LANES), F32).at[:, :n_experts].set(w_router)
    hn, idx, gate = pl.pallas_call(
        functools.partial(_rmsnorm_router_kernel, n_experts=n_experts),
        grid=(m // tr,),
        in_specs=[pl.BlockSpec((tr, d), lambda i: (i, 0)), pl.BlockSpec((1, d), lambda i: (0, 0)),
                  pl.BlockSpec((d, LANES), lambda i: (0, 0))],
        out_specs=[pl.BlockSpec((tr, d), lambda i: (i, 0)), pl.BlockSpec((tr, LANES), lambda i: (i, 0)),
                   pl.BlockSpec((tr, LANES), lambda i: (i, 0))],
        out_shape=[jax.ShapeDtypeStruct((m, d), F32), jax.ShapeDtypeStruct((m, LANES), jnp.int32),
                   jax.ShapeDtypeStruct((m, LANES), F32)],
        compiler_params=_cparams(("parallel",)),
        name="rmsnorm_router",
    )(x, g.reshape(1, d), wr)
    return hn, idx[:, :TOP_K], gate


def _epilogue(mode, accs, res_ref, o_ref):
    if mode == "swiglu":
        o_ref[...] = (_silu(accs[0]) * accs[1]).astype(o_ref.dtype)
    elif mode == "res":
        o_ref[...] = res_ref[...] + accs[0]
    else:
        o_ref[...] = accs[0].astype(o_ref.dtype)


def _mm_kernel(*refs, nk, mode):
    n_w = 2 if mode == "swiglu" else 1
    a_ref = refs[0]
    w_refs = refs[1:1 + n_w]
    pos = 1 + n_w
    res_ref = None
    if mode == "res":
        res_ref = refs[pos]
        pos += 1
    o_ref = refs[pos]
    acc_refs = refs[pos + 1:]

    a = a_ref[...]
    parts = [_dot(a, w[...]) for w in w_refs]
    if nk == 1:
        _epilogue(mode, parts, res_ref, o_ref)
        return
    k = pl.program_id(2)

    @pl.when(k == 0)
    def _():
        for acc, p in zip(acc_refs, parts):
            acc[...] = p

    @pl.when(k > 0)
    def _():
        for acc, p in zip(acc_refs, parts):
            acc[...] += p

    @pl.when(k == nk - 1)
    def _():
        _epilogue(mode, [acc[...] for acc in acc_refs], res_ref, o_ref)


def matmul(a, ws, *, mode="plain", res=None, out_dtype=F32, tm, tn, tk, name):
    m, kdim = a.shape
    n_out = ws[0].shape[1]
    nk = kdim // tk
    grid = (m // tm, n_out // tn, nk)
    in_specs = [pl.BlockSpec((tm, tk), lambda i, j, k: (i, k))]
    in_specs += [pl.BlockSpec((tk, tn), lambda i, j, k: (k, j)) for _ in ws]
    args = [a, *ws]
    if mode == "res":
        in_specs.append(pl.BlockSpec((tm, tn), lambda i, j, k: (i, j)))
        args.append(res)
    scratch = [pltpu.VMEM((tm, tn), F32) for _ in ws] if nk > 1 else []
    return pl.pallas_call(
        functools.partial(_mm_kernel, nk=nk, mode=mode),
        grid=grid,
        in_specs=in_specs,
        out_specs=pl.BlockSpec((tm, tn), lambda i, j, k: (i, j)),
        out_shape=jax.ShapeDtypeStruct((m, n_out), out_dtype),
        scratch_shapes=scratch,
        compiler_params=_cparams(("parallel", "parallel", "arbitrary")),
        name=name,
    )(*args)


def _mm_ws_kernel(*refs, mode):
    n_w = 2 if mode == "swiglu" else 1
    a_ref = refs[0]
    w_refs = refs[1:1 + n_w]
    o_ref = refs[1 + n_w]
    ws_refs = refs[2 + n_w:]

    @pl.when(pl.program_id(1) == 0)
    def _():
        for w, ws in zip(w_refs, ws_refs):
            ws[...] = w[...].astype(BF16)

    a = a_ref[...]
    _epilogue(mode, [_dot(a, ws[...]) for ws in ws_refs], None, o_ref)


def matmul_ws(a, ws, *, mode="plain", out_dtype=F32, n_out=None, col_off=0, tm, tn, name):
    m, kdim = a.shape
    n_out = ws[0].shape[1] if n_out is None else n_out
    grid = (n_out // tn, m // tm)
    in_specs = [pl.BlockSpec((tm, kdim), lambda j, i: (i, 0))]
    in_specs += [pl.BlockSpec((kdim, tn), lambda j, i: (0, j + col_off)) for _ in ws]
    return pl.pallas_call(
        functools.partial(_mm_ws_kernel, mode=mode),
        grid=grid,
        in_specs=in_specs,
        out_specs=pl.BlockSpec((tm, tn), lambda j, i: (i, j)),
        out_shape=jax.ShapeDtypeStruct((m, n_out), out_dtype),
        scratch_shapes=[pltpu.VMEM((kdim, tn), BF16) for _ in ws],
        compiler_params=_cparams(("parallel", "arbitrary")),
        name=name,
    )(a, *ws)


def _hgrn_chunk(q_raw, fz, v, g_raw, lb, ng, st, c):
    q = _silu(q_raw)
    f = lb + (1.0 - lb) * _sigmoid(fz)
    lf = jnp.log2(f)
    k = 1.0 - f
    row = lax.broadcasted_iota(jnp.int32, (c, c), 0)
    col = lax.broadcasted_iota(jnp.int32, (c, c), 1)
    tri = (row >= col).astype(BF16)
    hi, mid, lo = _split3(lf)
    b = _dot(tri, hi) + _dot(tri, mid) + _dot(tri, lo)
    b_end = b[c - 1:c, :]

    o = _dot_nt((q * jnp.exp2(b)).astype(BF16), st.astype(BF16))
    r8 = lax.broadcasted_iota(jnp.int32, (HALF, 1), 0)

    def pair_term(qq, bb, bs, ks, vs, mask_from):
        d = bb - bs
        if mask_from is not None:
            d = jnp.where(r8 >= mask_from, d, NEG_BIG)
        return jnp.sum(qq * jnp.exp2(d) * ks, axis=-1, keepdims=True) * vs

    outs = []
    for i in range(c // SUB):
        lo_r, md_r, hi_r = i * SUB, i * SUB + HALF, (i + 1) * SUB
        top, bot = slice(lo_r, md_r), slice(md_r, hi_r)
        ot, ob = o[top], o[bot]
        for s in range(HALF):
            bs, ks, vs = b[lo_r + s:lo_r + s + 1], k[lo_r + s:lo_r + s + 1], v[lo_r + s:lo_r + s + 1]
            ot = ot + pair_term(q[top], b[top], bs, ks, vs, s)
            ob = ob + pair_term(q[bot], b[bot], bs, ks, vs, None)
        for s in range(HALF):
            bs, ks, vs = b[md_r + s:md_r + s + 1], k[md_r + s:md_r + s + 1], v[md_r + s:md_r + s + 1]
            ob = ob + pair_term(q[bot], b[bot], bs, ks, vs, s)
        oi = jnp.concatenate([ot, ob], axis=0)
        if i > 0:
            anchor = b[lo_r - 1:lo_r, :]
            qt = (q[lo_r:hi_r] * jnp.exp2(b[lo_r:hi_r] - anchor)).astype(BF16)
            kt = (k[:lo_r] * jnp.exp2(anchor - b[:lo_r])).astype(BF16)
            sc = _dot_nt(qt, kt)
            oi = oi + _dot(sc.astype(BF16), v[:lo_r].astype(BF16))
        outs.append(oi)
    o = outs[0] if len(outs) == 1 else jnp.concatenate(outs, axis=0)

    kd = (k * jnp.exp2(b_end - b)).astype(BF16)
    st_new = st * jnp.exp2(b_end) + _dot_tn(v.astype(BF16), kd)

    o = o * lax.rsqrt(jnp.mean(o * o, axis=-1, keepdims=True) + EPS)
    return o * ng * _silu(g_raw), st_new


def _hgrn_prompt_kernel(q_ref, f_ref, v_ref, g_ref, lb_ref, ng_ref, o_ref, s_ref, st_ref, *, hb, c, dk):
    t = q_ref.shape[0]
    st_ref[...] = jnp.zeros_like(st_ref)

    def body(ci, carry):
        r0 = pl.multiple_of(ci * c, c)
        for h in range(hb):
            ls = slice(h * dk, (h + 1) * dk)
            o, st_new = _hgrn_chunk(q_ref[pl.ds(r0, c), ls], f_ref[pl.ds(r0, c), ls], v_ref[pl.ds(r0, c), ls],
                                    g_ref[pl.ds(r0, c), ls], lb_ref[:, ls], ng_ref[:, ls], st_ref[h], c)
            st_ref[h] = st_new
            o_ref[pl.ds(r0, c), ls] = o.astype(o_ref.dtype)
        return carry

    lax.fori_loop(0, t // c, body, 0, unroll=4)
    for h in range(hb):
        s_ref[0, h] = st_ref[h].T


def _hgrn_sample_kernel(q_ref, f_ref, v_ref, g_ref, lb_ref, ng_ref, s0_ref, o_ref, s_ref, *, hb, c, dk):
    nb = s0_ref.shape[0]

    def body(bi, carry):
        r0 = pl.multiple_of(bi * c, c)
        for h in range(hb):
            ls = slice(h * dk, (h + 1) * dk)
            o, st_new = _hgrn_chunk(q_ref[pl.ds(r0, c), ls], f_ref[pl.ds(r0, c), ls], v_ref[pl.ds(r0, c), ls],
                                    g_ref[pl.ds(r0, c), ls], lb_ref[:, ls], ng_ref[:, ls], s0_ref[bi, h].T, c)
            s_ref[bi, h] = st_new.T
            o_ref[pl.ds(r0, c), ls] = o.astype(o_ref.dtype)
        return carry

    lax.fori_loop(0, nb, body, 0, unroll=2)


def hgrn_scan(proj, lb, ng, s0, *, row0, nbatch, t, heads, dk, hb):
    w = hb * dk
    nhb = heads // hb
    lb2 = lb.reshape(1, heads * dk)
    ng2 = ng.reshape(1, heads * dk)
    if s0 is None:
        c = min(CHUNK, t)
        rb0 = row0 // t
        kern = functools.partial(_hgrn_prompt_kernel, hb=hb, c=c, dk=dk)
        col = lambda part: pl.BlockSpec((t, w), lambda b, j, part=part: (rb0 + b, part * nhb + j))
        vec = pl.BlockSpec((1, w), lambda b, j: (0, j))
        return pl.pallas_call(
            kern,
            grid=(nbatch, nhb),
            in_specs=[col(0), col(1), col(2), col(3), vec, vec],
            out_specs=[pl.BlockSpec((t, w), lambda b, j: (b, j)),
                       pl.BlockSpec((1, hb, dk, dk), lambda b, j: (b, j, 0, 0))],
            out_shape=[jax.ShapeDtypeStruct((nbatch * t, heads * dk), BF16),
                       jax.ShapeDtypeStruct((nbatch, heads, dk, dk), F32)],
            scratch_shapes=[pltpu.VMEM((hb, dk, dk), F32)],
            compiler_params=_cparams(("parallel", "parallel")),
            name="hgrn_prompt",
        )(proj, proj, proj, proj, lb2, ng2)
    rows = nbatch * t
    rb0 = row0 // rows
    kern = functools.partial(_hgrn_sample_kernel, hb=hb, c=t, dk=dk)
    col = lambda part: pl.BlockSpec((rows, w), lambda j, part=part: (rb0, part * nhb + j))
    vec = pl.BlockSpec((1, w), lambda j: (0, j))
    return pl.pallas_call(
        kern,
        grid=(nhb,),
        in_specs=[col(0), col(1), col(2), col(3), vec, vec,
                  pl.BlockSpec((nbatch, hb, dk, dk), lambda j: (0, j, 0, 0))],
        out_specs=[pl.BlockSpec((rows, w), lambda j: (0, j)),
                   pl.BlockSpec((nbatch, hb, dk, dk), lambda j: (0, j, 0, 0))],
        out_shape=[jax.ShapeDtypeStruct((rows, heads * dk), BF16),
                   jax.ShapeDtypeStruct((nbatch, heads, dk, dk), F32)],
        compiler_params=_cparams(("parallel",)),
        name="hgrn_sample",
    )(proj, proj, proj, proj, lb2, ng2, s0)


def _conv_silu(x, pad_ref, w_ref, bias_ref, c):
    pad_ref[pl.ds(8, c), :] = x
    acc = bias_ref[...] + w_ref[MB_CONV - 1:MB_CONV, :] * x
    for j in range(MB_CONV - 1):
        acc = acc + w_ref[j:j + 1, :] * pad_ref[pl.ds(5 + j, c), :]
    pad_ref[pl.ds(5, 3), :] = pad_ref[pl.ds(5 + c, 3), :]
    return _silu(acc)


def _ssd_kernel(z_ref, x_ref, b_ref, c_ref, dt_ref, *rest, c, heads, has_state):
    if has_state:
        conv0x_ref, conv0b_ref, conv0c_ref, s0_ref = rest[:4]
        rest = rest[4:]
    (wx_ref, wb_ref, wc_ref, bx_ref, bb_ref, bc_ref, dtb_ref, alog_ref, dsk_ref, ng_ref, ep_ref, ec_ref,
     y_ref, s_ref, st_ref, padx_ref, padb_ref, padc_ref) = rest
    p = MB_HEADDIM
    n = MB_DSTATE
    hpg = heads // MB_GROUPS
    gw = hpg * p
    ci = pl.program_id(1)
    nci = pl.num_programs(1)

    @pl.when(ci == 0)
    def _():
        if has_state:
            padx_ref[pl.ds(5, 3), :] = conv0x_ref[0]
            padb_ref[pl.ds(5, 3), :] = conv0b_ref[0]
            padc_ref[pl.ds(5, 3), :] = conv0c_ref[0]
            st_ref[...] = s0_ref[0].T
        else:
            padx_ref[pl.ds(5, 3), :] = jnp.zeros((3, padx_ref.shape[1]), F32)
            padb_ref[pl.ds(5, 3), :] = jnp.zeros((3, padb_ref.shape[1]), F32)
            padc_ref[pl.ds(5, 3), :] = jnp.zeros((3, padc_ref.shape[1]), F32)
            st_ref[...] = jnp.zeros_like(st_ref)

    bm = _conv_silu(b_ref[...], padb_ref, wb_ref, bb_ref, c)
    cm = _conv_silu(c_ref[...], padc_ref, wc_ref, bc_ref, c)

    dtr = dt_ref[...] + dtb_ref[...]
    dt = jnp.maximum(dtr, 0.0) + jnp.log1p(jnp.exp(-jnp.abs(dtr)))
    la = dt * (-jnp.exp(alog_ref[...]))
    row = lax.broadcasted_iota(jnp.int32, (c, c), 0)
    col = lax.broadcasted_iota(jnp.int32, (c, c), 1)
    tri = (row >= col).astype(BF16)
    hi, mid, lo = _split3(la)
    cs = _dot(tri, hi) + _dot(tri, mid) + _dot(tri, lo)
    cs3 = _split3(cs)
    dt3 = _split3(dt)

    mw = hpg * c
    t_m = lax.broadcasted_iota(jnp.int32, (c, mw), 0)
    s_m = lax.broadcasted_iota(jnp.int32, (c, mw), 1) % c
    nbk = LANES // c
    bd_rows = lax.broadcasted_iota(jnp.int32, (LANES, nbk * p), 0) // c
    bd_cols = lax.broadcasted_iota(jnp.int32, (LANES, nbk * p), 1) // p
    bd_mask = bd_rows == bd_cols

    for g in range(MB_GROUPS):
        xl = slice(g * gw, (g + 1) * gw)
        nl = slice(g * n, (g + 1) * n)
        xs = _conv_silu(x_ref[:, xl], padx_ref.at[:, xl], wx_ref.at[:, xl], bx_ref.at[:, xl], c)
        bg = bm[:, nl]
        cg = cm[:, nl]
        ep = ep_ref[:, xl]
        cse = _dot(cs3[0], ep) + _dot(cs3[1], ep) + _dot(cs3[2], ep)
        dte = _dot(dt3[0], ep) + _dot(dt3[1], ep) + _dot(dt3[2], ep)
        cs_end = cse[c - 1:c, :]
        xdt = xs * dte
        xw = xdt * jnp.exp(cs_end - cse)

        st_g = st_ref[:, xl]
        y = _dot(cg.astype(BF16), st_g.astype(BF16)) * jnp.exp(cse)
        st_ref[:, xl] = st_g * jnp.exp(cs_end) + _dot_tn(bg.astype(BF16), xw.astype(BF16))

        ec = ec_ref[:, g * mw:(g + 1) * mw]
        csm = _dot(cs3[0], ec) + _dot(cs3[1], ec) + _dot(cs3[2], ec)
        cs_s = jnp.sum(jnp.where(t_m == s_m, csm, 0.0), axis=0, keepdims=True)
        lw = jnp.exp(jnp.where(t_m >= s_m, csm - cs_s, NEG_BIG))
        b_t = jnp.concatenate([bg] * hpg, axis=0).astype(BF16)
        mm = (_dot_nt(cg.astype(BF16), b_t) * lw).astype(BF16)
        xdt_b = xdt.astype(BF16)
        parts = []
        for blk in range(mw // LANES):
            xb = xdt_b[:, blk * nbk * p:(blk + 1) * nbk * p]
            bd = jnp.where(bd_mask, jnp.concatenate([xb] * nbk, axis=0), jnp.zeros((), BF16))
            parts.append(_dot(mm[:, blk * LANES:(blk + 1) * LANES], bd))
        y = y + jnp.concatenate(parts, axis=1) + dsk_ref[:, xl] * xs

        y = y * _silu(z_ref[:, xl])
        y = y * lax.rsqrt(jnp.mean(y * y, axis=-1, keepdims=True) + EPS)
        y_ref[:, xl] = (y * ng_ref[:, xl]).astype(y_ref.dtype)

    @pl.when(ci == nci - 1)
    def _():
        s_ref[0] = st_ref[...].T


def ssd_scan(zx, dt_raw, conv0, s0, conv_w, conv_b, dt_bias, a_log, d_skip, norm_g, *, row0, nbatch, t, heads):
    p, n = MB_HEADDIM, MB_DSTATE
    di = heads * p
    gn = MB_GROUPS * n
    c = min(CHUNK, t)
    nch = t // c
    rb0 = row0 // c
    has_state = s0 is not None
    head_of_x = jnp.arange(di) // p
    ep = (jnp.arange(heads)[:, None] == head_of_x[None, :]).astype(BF16)
    head_of_m = jnp.arange(heads * c) // c
    ec = (jnp.arange(heads)[:, None] == head_of_m[None, :]).astype(BF16)
    dsk = jnp.repeat(d_skip.astype(F32), p).reshape(1, di)

    row = lambda w, blk: pl.BlockSpec((c, w), lambda b, k, blk=blk: (rb0 + b * nch + k, blk))
    full = lambda a: pl.BlockSpec(a.shape, lambda b, k: (0,) * a.ndim)
    in_specs = [row(di, 0), row(di, 1), row(gn, 2 * di // gn), row(gn, 2 * di // gn + 1), row(heads, 0)]
    args = [zx, zx, zx, zx, dt_raw]
    if has_state:
        in_specs += [pl.BlockSpec((1, MB_CONV - 1, di), lambda b, k: (b, 0, 0)),
                     pl.BlockSpec((1, MB_CONV - 1, gn), lambda b, k: (b, 0, 0)),
                     pl.BlockSpec((1, MB_CONV - 1, gn), lambda b, k: (b, 0, 0)),
                     pl.BlockSpec((1, di, n), lambda b, k: (b, 0, 0))]
        args += [conv0[:, :, :di], conv0[:, :, di:di + gn], conv0[:, :, di + gn:], s0.reshape(nbatch, di, n)]
    consts = [conv_w[:, :di], conv_w[:, di:di + gn], conv_w[:, di + gn:],
              conv_b[:di].reshape(1, di), conv_b[di:di + gn].reshape(1, gn), conv_b[di + gn:].reshape(1, gn),
              dt_bias.reshape(1, heads), a_log.reshape(1, heads), dsk, norm_g.reshape(1, di), ep, ec]
    in_specs += [full(a) for a in consts]
    args += consts
    y, s_fin = pl.pallas_call(
        functools.partial(_ssd_kernel, c=c, heads=heads, has_state=has_state),
        grid=(nbatch, nch),
        in_specs=in_specs,
        out_specs=[pl.BlockSpec((c, di), lambda b, k: (b * nch + k, 0)),
                   pl.BlockSpec((1, di, n), lambda b, k: (b, 0, 0))],
        out_shape=[jax.ShapeDtypeStruct((nbatch * t, di), BF16), jax.ShapeDtypeStruct((nbatch, di, n), F32)],
        scratch_shapes=[pltpu.VMEM((n, di), F32), pltpu.VMEM((c + 8, di), F32),
                        pltpu.VMEM((c + 8, gn), F32), pltpu.VMEM((c + 8, gn), F32)],
        compiler_params=_cparams(("parallel", "arbitrary")),
        name="ssd_state" if has_state else "ssd_prompt",
    )(*args)
    return y, s_fin.reshape(nbatch, heads, p, n)


def _gather_rows_kernel(idx_ref, src_ref, o_ref, buf_ref, sem, *, tg):
    i = pl.program_id(0)

    def copy(r):
        return pltpu.make_async_copy(src_ref.at[pl.ds(idx_ref[i * tg + r], 1)], buf_ref.at[pl.ds(r, 1)], sem)

    def start(r, carry):
        copy(r).start()
        return carry

    def wait(r, carry):
        copy(r).wait()
        return carry

    lax.fori_loop(0, tg, start, 0)
    lax.fori_loop(0, tg, wait, 0)
    o_ref[...] = buf_ref[...].astype(o_ref.dtype)


def gather_rows(src, idx, out_dtype, tg=256):
    rows = idx.shape[0]
    d = src.shape[1]
    return pl.pallas_call(
        functools.partial(_gather_rows_kernel, tg=tg),
        grid_spec=pltpu.PrefetchScalarGridSpec(
            num_scalar_prefetch=1,
            grid=(rows // tg,),
            in_specs=[pl.BlockSpec(memory_space=pl.ANY)],
            out_specs=pl.BlockSpec((tg, d), lambda i, idx: (i, 0)),
            scratch_shapes=[pltpu.VMEM((tg, d), src.dtype), pltpu.SemaphoreType.DMA(())],
        ),
        out_shape=jax.ShapeDtypeStruct((rows, d), out_dtype),
        compiler_params=_cparams(("arbitrary",)),
        name="moe_gather",
    )(idx, src)


def _expert_rows_pipeline(first_ref, ntile_ref, in_hbm, out_hbm, inbuf, outbuf, insem, outsem, *, tmg, tn, n_experts,
                          prepare, compute):
    j = pl.program_id(0)
    e = pl.program_id(1)
    t0 = first_ref[e]
    nt = ntile_ref[e]
    col0 = pl.multiple_of(j * tn, tn)
    is_expert = e < n_experts

    def in_copy(t, slot):
        r0 = pl.multiple_of((t0 + t) * tmg, tmg)
        return pltpu.make_async_copy(in_hbm.at[pl.ds(r0, tmg)], inbuf.at[slot], insem.at[slot])

    def out_copy(t, slot):
        r0 = pl.multiple_of((t0 + t) * tmg, tmg)
        return pltpu.make_async_copy(outbuf.at[slot], out_hbm.at[pl.ds(r0, tmg), pl.ds(col0, tn)], outsem.at[slot])

    @pl.when(jnp.logical_and(is_expert, nt > 0))
    def _():
        in_copy(0, 0).start(priority=1)
        prepare()

    def body(t, carry):
        slot = t % 2

        @pl.when(t >= 2)
        def _():
            out_copy(t - 2, slot).wait()

        @pl.when(is_expert)
        def _():
            in_copy(t, slot).wait()

            @pl.when(t + 1 < nt)
            def _():
                in_copy(t + 1, 1 - slot).start(priority=1)

            outbuf[slot] = compute(inbuf[slot]).astype(outbuf.dtype)

        @pl.when(jnp.logical_not(is_expert))
        def _():
            outbuf[slot] = jnp.zeros(outbuf.shape[1:], outbuf.dtype)

        out_copy(t, slot).start()
        return carry

    lax.fori_loop(0, nt, body, 0)

    @pl.when(nt >= 2)
    def _():
        out_copy(nt - 2, nt % 2).wait()

    @pl.when(nt >= 1)
    def _():
        out_copy(nt - 1, (nt - 1) % 2).wait()


def _moe_ffn1_kernel(first_ref, ntile_ref, w1_ref, w3_ref, x_hbm, o_hbm, w1s, w3s, xbuf, obuf, xsem, osem, **kw):
    def prepare():
        w1s[...] = w1_ref[0].astype(BF16)
        w3s[...] = w3_ref[0].astype(BF16)

    def compute(a):
        return _silu(_dot(a, w1s[...])) * _dot(a, w3s[...])

    _expert_rows_pipeline(first_ref, ntile_ref, x_hbm, o_hbm, xbuf, obuf, xsem, osem, prepare=prepare,
                          compute=compute, **kw)


def _moe_ffn2_kernel(first_ref, ntile_ref, w2_ref, h_hbm, o_hbm, hbuf, obuf, hsem, osem, **kw):
    _expert_rows_pipeline(first_ref, ntile_ref, h_hbm, o_hbm, hbuf, obuf, hsem, osem, prepare=lambda: None,
                          compute=lambda a: _dot(a, w2_ref[0]), **kw)


def _expert_rows_call(kern, rows, weights, tile_first, tile_count, *, n_out, out_dtype, tmg, tn, extra_scratch, name):
    rp, kdim = rows.shape
    n_experts = weights[0].shape[0]

    def w_map(j, e, first, count):
        return (jnp.minimum(e, n_experts - 1), 0, j)

    return pl.pallas_call(
        functools.partial(kern, tmg=tmg, tn=tn, n_experts=n_experts),
        grid_spec=pltpu.PrefetchScalarGridSpec(
            num_scalar_prefetch=2,
            grid=(n_out // tn, n_experts + 1),
            in_specs=[pl.BlockSpec((1, kdim, tn), w_map) for _ in weights] + [pl.BlockSpec(memory_space=pl.ANY)],
            out_specs=pl.BlockSpec(memory_space=pl.ANY),
            scratch_shapes=extra_scratch + [pltpu.VMEM((2, tmg, kdim), BF16), pltpu.VMEM((2, tmg, tn), out_dtype),
                                            pltpu.SemaphoreType.DMA((2,)), pltpu.SemaphoreType.DMA((2,))],
        ),
        out_shape=jax.ShapeDtypeStruct((rp, n_out), out_dtype),
        compiler_params=_cparams(("arbitrary", "arbitrary")),
        name=name,
    )(tile_first, tile_count, *weights, rows)


def moe_ffn1(xs, w1, w3, tile_first, tile_count, *, tmg, tn):
    d, dff = w1.shape[1], w1.shape[2]
    return _expert_rows_call(_moe_ffn1_kernel, xs, [w1, w3], tile_first, tile_count, n_out=dff, out_dtype=BF16,
                             tmg=tmg, tn=tn, extra_scratch=[pltpu.VMEM((d, tn), BF16), pltpu.VMEM((d, tn), BF16)],
                             name="moe_ffn1")


def moe_ffn2(h, w2, tile_first, tile_count, *, tmg, tn):
    return _expert_rows_call(_moe_ffn2_kernel, h, [w2], tile_first, tile_count, n_out=w2.shape[2], out_dtype=F32,
                             tmg=tmg, tn=tn, extra_scratch=[], name="moe_ffn2")


def _combine_norm_kernel(dest_ref, x_ref, gate_ref, y_ref, g_ref, o_ref, buf_ref, sem, *, tc):
    i = pl.program_id(0)

    def copy(r):
        row = dest_ref[i * TOP_K * tc + r]
        return pltpu.make_async_copy(y_ref.at[pl.ds(row, 1)], buf_ref.at[pl.ds(r, 1)], sem)

    def start(r, carry):
        copy(r).start()
        return carry

    def wait(r, carry):
        copy(r).wait()
        return carry

    lax.fori_loop(0, TOP_K * tc, start, 0)
    lax.fori_loop(0, TOP_K * tc, wait, 0)
    x = x_ref[...]
    gate = gate_ref[...]
    for kk in range(TOP_K):
        x = x + gate[:, kk:kk + 1] * buf_ref[pl.ds(kk * tc, tc), :]
    ms = jnp.mean(x * x, axis=-1, keepdims=True)
    o_ref[...] = x * lax.rsqrt(ms + EPS) * g_ref[...]


def combine_norm(x, gate, y_rows, dest, g, tc=128):
    m, d = x.shape
    return pl.pallas_call(
        functools.partial(_combine_norm_kernel, tc=tc),
        grid_spec=pltpu.PrefetchScalarGridSpec(
            num_scalar_prefetch=1,
            grid=(m // tc,),
            in_specs=[pl.BlockSpec((tc, d), lambda i, dest: (i, 0)), pl.BlockSpec((tc, LANES), lambda i, dest: (i, 0)),
                      pl.BlockSpec(memory_space=pl.ANY), pl.BlockSpec((1, d), lambda i, dest: (0, 0))],
            out_specs=pl.BlockSpec((tc, d), lambda i, dest: (i, 0)),
            scratch_shapes=[pltpu.VMEM((TOP_K * tc, d), F32), pltpu.SemaphoreType.DMA(())],
        ),
        out_shape=jax.ShapeDtypeStruct((m, d), F32),
        compiler_params=_cparams(("arbitrary",)),
        name="moe_combine_norm",
    )(dest, x, gate, y_rows, g.reshape(1, d))


def moe_dispatch(top_i, n_experts, tmg):
    m = top_i.shape[0]
    na = m * TOP_K
    nt = na // tmg + n_experts
    flat_e = top_i.reshape(na).astype(jnp.int32)
    order = jnp.argsort(flat_e, stable=True).astype(jnp.int32)
    counts = jnp.zeros((n_experts,), jnp.int32).at[flat_e].add(1)
    padded = (counts + tmg - 1) // tmg * tmg
    pad_end = jnp.cumsum(padded)
    pad_start = pad_end - padded
    cnt_start = jnp.cumsum(counts) - counts
    sorted_e = flat_e[order]
    dest_sorted = pad_start[sorted_e] + jnp.arange(na, dtype=jnp.int32) - cnt_start[sorted_e]
    row_token = jnp.zeros((nt * tmg,), jnp.int32).at[dest_sorted].set(order // TOP_K)
    dest = jnp.zeros((na,), jnp.int32).at[order].set(dest_sorted)
    n_used = (pad_end[-1:] // tmg).astype(jnp.int32)
    tile_first = jnp.concatenate([pad_start // tmg, n_used]).astype(jnp.int32)
    tile_count = jnp.concatenate([padded // tmg, nt - n_used]).astype(jnp.int32)
    return row_token, dest, tile_first, tile_count


def kernel(x_prompt, x_sample, state_hgrn, state_ssm, state_conv, norm_mix, norm_ffn, norm_final, hg_w_in, hg_lb_logits, hg_norm, hg_w_out, mb_w_in, mb_conv_w, mb_conv_b, mb_dt_bias, mb_a_log, mb_d, mb_norm, mb_w_out, ffn_w1, ffn_w3, ffn_w2, moe_router, moe_w1, moe_w3, moe_w2):
    bp, tp, d = x_prompt.shape
    bs, ts, _ = x_sample.shape
    mp, ms = bp * tp, bs * ts
    m = mp + ms
    hg_heads, hg_dk = state_hgrn.shape[2], state_hgrn.shape[3]
    mb_heads = state_ssm.shape[2]
    di = mb_heads * MB_HEADDIM
    conv_dim = state_conv.shape[3]
    n_experts = moe_router.shape[2]
    tm = m // 8
    tm_ws = m // 16

    x = jnp.concatenate([x_prompt.reshape(mp, d), x_sample.reshape(ms, d)], axis=0)
    bf = lambda w: w.astype(BF16)

    lbs = jnp.cumsum(jax.nn.softmax(hg_lb_logits.astype(F32), axis=0), axis=0)
    hn = rmsnorm(x, norm_mix[0], BF16)
    proj = matmul_ws(hn, [hg_w_in[0]], tm=tm_ws, tn=1024, name="hg_in")
    scan = functools.partial(hgrn_scan, proj, lbs[0], hg_norm[0], heads=hg_heads, dk=hg_dk, hb=2)
    o_p, hg_p = scan(None, row0=0, nbatch=bp, t=tp)
    o_s, hg_s = scan(state_hgrn[0], row0=mp, nbatch=bs, t=ts)
    o = jnp.concatenate([o_p, o_s], axis=0)
    x = matmul(o, [bf(hg_w_out[0])], mode="res", res=x, tm=tm, tn=1024, tk=2048, name="hg_out")
    hn = rmsnorm(x, norm_ffn[0], BF16)
    h = matmul_ws(hn, [ffn_w1[0], ffn_w3[0]], mode="swiglu", out_dtype=BF16, tm=tm_ws, tn=512, name="ffn_in")
    x = matmul(h, [bf(ffn_w2[0])], mode="res", res=x, tm=tm, tn=1024, tk=2048, name="ffn_out")

    hn = rmsnorm(x, norm_mix[1], BF16)
    zx = matmul_ws(hn, [mb_w_in[0]], n_out=di + conv_dim, tm=tm_ws, tn=1024, name="mb_in")
    dt_raw = matmul_ws(hn, [mb_w_in[0]], n_out=mb_heads, col_off=(di + conv_dim) // mb_heads, tm=tm,
                       tn=mb_heads, name="mb_in_dt")
    ssd = functools.partial(ssd_scan, zx, dt_raw, conv_w=mb_conv_w[0], conv_b=mb_conv_b[0], dt_bias=mb_dt_bias[0],
                            a_log=mb_a_log[0], d_skip=mb_d[0], norm_g=mb_norm[0], heads=mb_heads)
    y_p, ssm_p = ssd(None, None, row0=0, nbatch=bp, t=tp)
    y_s, ssm_s = ssd(state_conv[0], state_ssm[0], row0=mp, nbatch=bs, t=ts)
    keep = MB_CONV - 1
    last_rows = lambda r0, nb, t: (r0 + jnp.arange(nb)[:, None] * t + (t - keep) + jnp.arange(keep)[None, :]).reshape(-1)
    conv_p = jnp.take(zx, last_rows(0, bp, tp), axis=0)[:, di:].reshape(bp, keep, conv_dim)
    conv_s = jnp.take(zx, last_rows(mp, bs, ts), axis=0)[:, di:].reshape(bs, keep, conv_dim)
    y = jnp.concatenate([y_p, y_s], axis=0)
    x = matmul(y, [bf(mb_w_out[0])], mode="res", res=x, tm=tm, tn=1024, tk=2048, name="mb_out")

    tmg = 256
    hn32, top_i, gates = rmsnorm_router(x, norm_ffn[1], moe_router[0])
    row_token, dest, tile_first, tile_count = moe_dispatch(top_i, n_experts, tmg)
    xs = gather_rows(hn32, row_token, BF16)
    hmid = moe_ffn1(xs, moe_w1[0], moe_w3[0], tile_first, tile_count, tmg=tmg, tn=512)
    yr = moe_ffn2(hmid, bf(moe_w2[0]), tile_first, tile_count, tmg=tmg, tn=512)
    tc = 128
    dest_tiles = dest.reshape(m // tc, tc, TOP_K).transpose(0, 2, 1).reshape(-1)
    yf = combine_norm(x, gates, yr, dest_tiles, norm_final, tc=tc)

    y_prompt = yf[:mp].reshape(bp, tp, d)
    y_sample = yf[mp:].reshape(bs, ts, d)
    return (y_prompt, y_sample, hg_p[None], hg_s[None], ssm_p[None], ssm_s[None], conv_p[None], conv_s[None])
```

```python
import functools

import jax
import jax.numpy as jnp
from jax import lax
from jax.experimental import pallas as pl
from jax.experimental.pallas import tpu as pltpu

EPS = 1e-5
CHUNK = 64
SUB = 16
HALF = SUB // 2
MB_GROUPS = 8
MB_HEADDIM = 64
MB_DSTATE = 128
MB_CONV = 4
TOP_K = 2
NEG_BIG = -1e30
IN_AHEAD = 2
IN_SLOTS = IN_AHEAD + 1

V7X_VMEM_BYTES = 64 * 1024 * 1024
VMEM_LIMIT = V7X_VMEM_BYTES - 4 * 1024 * 1024
LANES = 128

F32 = jnp.float32
BF16 = jnp.bfloat16


def _cparams(sem):
    return pltpu.CompilerParams(dimension_semantics=sem, vmem_limit_bytes=VMEM_LIMIT)


def _sigmoid(x):
    return 1.0 / (1.0 + jnp.exp(-x))


def _silu(x):
    return x * _sigmoid(x)


def _split3(x):
    hi = x.astype(BF16)
    r = x - hi.astype(F32)
    mid = r.astype(BF16)
    lo = (r - mid.astype(F32)).astype(BF16)
    return hi, mid, lo


def _dot(a, b):
    return jnp.dot(a, b, preferred_element_type=F32)


def _dot_nt(a, b):
    return lax.dot_general(a, b, (((1,), (1,)), ((), ())), preferred_element_type=F32)


def _dot_tn(a, b):
    return lax.dot_general(a, b, (((0,), (0,)), ((), ())), preferred_element_type=F32)


def _rmsnorm_kernel(x_ref, g_ref, o_ref):
    x = x_ref[...]
    ms = jnp.mean(x * x, axis=-1, keepdims=True)
    o_ref[...] = (x * lax.rsqrt(ms + EPS) * g_ref[...]).astype(o_ref.dtype)


def rmsnorm(x, g, out_dtype, tr=256):
    m, d = x.shape
    return pl.pallas_call(
        _rmsnorm_kernel,
        grid=(m // tr,),
        in_specs=[pl.BlockSpec((tr, d), lambda i: (i, 0)), pl.BlockSpec((1, d), lambda i: (0, 0))],
        out_specs=pl.BlockSpec((tr, d), lambda i: (i, 0)),
        out_shape=jax.ShapeDtypeStruct((m, d), out_dtype),
        compiler_params=_cparams(("parallel",)),
        name="rmsnorm",
    )(x, g.reshape(1, d))


def _rmsnorm_router_kernel(x_ref, g_ref, wr_ref, o_ref, idx_ref, gate_ref, *, n_experts):
    x = x_ref[...]
    ms = jnp.mean(x * x, axis=-1, keepdims=True)
    hn = x * lax.rsqrt(ms + EPS) * g_ref[...]
    o_ref[...] = hn
    logits = jnp.dot(hn, wr_ref[...], preferred_element_type=F32, precision=lax.Precision.HIGHEST)
    lane = lax.broadcasted_iota(jnp.int32, logits.shape, 1).astype(F32)
    logits = jnp.where(lane < n_experts, logits, -jnp.inf)
    m1 = jnp.max(logits, axis=-1, keepdims=True)
    i1 = jnp.min(jnp.where(logits == m1, lane, float(LANES)), axis=-1, keepdims=True)
    rest = jnp.where(lane == i1, -jnp.inf, logits)
    m2 = jnp.max(rest, axis=-1, keepdims=True)
    i2 = jnp.min(jnp.where(rest == m2, lane, float(LANES)), axis=-1, keepdims=True)
    e2 = jnp.exp(m2 - m1)
    g1 = 1.0 / (1.0 + e2)
    g2 = e2 / (1.0 + e2)
    idx_ref[...] = jnp.where(lane == 0.0, i1, jnp.where(lane == 1.0, i2, 0.0)).astype(jnp.int32)
    gate_ref[...] = jnp.where(lane == 0.0, g1, jnp.where(lane == 1.0, g2, 0.0))


def rmsnorm_router(x, g, w_router, tr=256):
    m, d = x.shape
    n_experts = w_router.shape[1]
    wr = jnp.zeros((d, LANES), F32).at[:, :n_experts].set(w_router)
    hn, idx, gate = pl.pallas_call(
        functools.partial(_rmsnorm_router_kernel, n_experts=n_experts),
        grid=(m // tr,),
        in_specs=[pl.BlockSpec((tr, d), lambda i: (i, 0)), pl.BlockSpec((1, d), lambda i: (0, 0)),
                  pl.BlockSpec((d, LANES), lambda i: (0, 0))],
        out_specs=[pl.BlockSpec((tr, d), lambda i: (i, 0)), pl.BlockSpec((tr, LANES), lambda i: (i, 0)),
                   pl.BlockSpec((tr, LANES), lambda i: (i, 0))],
        out_shape=[jax.ShapeDtypeStruct((m, d), F32), jax.ShapeDtypeStruct((m, LANES), jnp.int32),
                   jax.ShapeDtypeStruct((m, LANES), F32)],
        compiler_params=_cparams(("parallel",)),
        name="rmsnorm_router",
    )(x, g.reshape(1, d), wr)
    return hn, idx[:, :TOP_K], gate


def _epilogue(mode, accs, res_ref, o_ref):
    if mode == "swiglu":
        o_ref[...] = (_silu(accs[0]) * accs[1]).astype(o_ref.dtype)
    elif mode == "res":
        o_ref[...] = res_ref[...] + accs[0]
    else:
        o_ref[...] = accs[0].astype(o_ref.dtype)


def _mm_kernel(*refs, nk, mode):
    n_w = 2 if mode == "swiglu" else 1
    a_ref = refs[0]
    w_refs = refs[1:1 + n_w]
    pos = 1 + n_w
    res_ref = None
    if mode == "res":
        res_ref = refs[pos]
        pos += 1
    o_ref = refs[pos]
    acc_refs = refs[pos + 1:]

    a = a_ref[...]
    parts = [_dot(a, w[...]) for w in w_refs]
    if nk == 1:
        _epilogue(mode, parts, res_ref, o_ref)
        return
    k = pl.program_id(2)

    @pl.when(k == 0)
    def _():
        for acc, p in zip(acc_refs, parts):
            acc[...] = p

    @pl.when(k > 0)
    def _():
        for acc, p in zip(acc_refs, parts):
            acc[...] += p

    @pl.when(k == nk - 1)
    def _():
        _epilogue(mode, [acc[...] for acc in acc_refs], res_ref, o_ref)


def matmul(a, ws, *, mode="plain", res=None, out_dtype=F32, tm, tn, tk, name):
    m, kdim = a.shape
    n_out = ws[0].shape[1]
    nk = kdim // tk
    grid = (m // tm, n_out // tn, nk)
    in_specs = [pl.BlockSpec((tm, tk), lambda i, j, k: (i, k))]
    in_specs += [pl.BlockSpec((tk, tn), lambda i, j, k: (k, j)) for _ in ws]
    args = [a, *ws]
    if mode == "res":
        in_specs.append(pl.BlockSpec((tm, tn), lambda i, j, k: (i, j)))
        args.append(res)
    scratch = [pltpu.VMEM((tm, tn), F32) for _ in ws] if nk > 1 else []
    return pl.pallas_call(
        functools.partial(_mm_kernel, nk=nk, mode=mode),
        grid=grid,
        in_specs=in_specs,
        out_specs=pl.BlockSpec((tm, tn), lambda i, j, k: (i, j)),
        out_shape=jax.ShapeDtypeStruct((m, n_out), out_dtype),
        scratch_shapes=scratch,
        compiler_params=_cparams(("parallel", "parallel", "arbitrary")),
        name=name,
    )(*args)


def _mm_ws_kernel(*refs, mode):
    n_w = 2 if mode == "swiglu" else 1
    a_ref = refs[0]
    w_refs = refs[1:1 + n_w]
    o_ref = refs[1 + n_w]
    ws_refs = refs[2 + n_w:]

    @pl.when(pl.program_id(1) == 0)
    def _():
        for w, ws in zip(w_refs, ws_refs):
            ws[...] = w[...].astype(BF16)

    a = a_ref[...]
    _epilogue(mode, [_dot(a, ws[...]) for ws in ws_refs], None, o_ref)


def matmul_ws(a, ws, *, mode="plain", out_dtype=F32, n_out=None, col_off=0, tm, tn, name):
    m, kdim = a.shape
    n_out = ws[0].shape[1] if n_out is None else n_out
    grid = (n_out // tn, m // tm)
    in_specs = [pl.BlockSpec((tm, kdim), lambda j, i: (i, 0))]
    in_specs += [pl.BlockSpec((kdim, tn), lambda j, i: (0, j + col_off)) for _ in ws]
    return pl.pallas_call(
        functools.partial(_mm_ws_kernel, mode=mode),
        grid=grid,
        in_specs=in_specs,
        out_specs=pl.BlockSpec((tm, tn), lambda j, i: (i, j)),
        out_shape=jax.ShapeDtypeStruct((m, n_out), out_dtype),
        scratch_shapes=[pltpu.VMEM((kdim, tn), BF16) for _ in ws],
        compiler_params=_cparams(("parallel", "arbitrary")),
        name=name,
    )(a, *ws)


def _hgrn_chunk(q_raw, fz, v, g_raw, lb, ng, st, c):
    q = _silu(q_raw)
    f = lb + (1.0 - lb) * _sigmoid(fz)
    lf = jnp.log2(f)
    k = 1.0 - f
    row = lax.broadcasted_iota(jnp.int32, (c, c), 0)
    col = lax.broadcasted_iota(jnp.int32, (c, c), 1)
    tri = (row >= col).astype(BF16)
    hi, mid, lo = _split3(lf)
    b = _dot(tri, hi) + _dot(tri, mid) + _dot(tri, lo)
    b_end = b[c - 1:c, :]

    o = _dot_nt((q * jnp.exp2(b)).astype(BF16), st.astype(BF16))
    r8 = lax.broadcasted_iota(jnp.int32, (HALF, 1), 0)

    def pair_term(qq, bb, bs, ks, vs, mask_from):
        d = bb - bs
        if mask_from is not None:
            d = jnp.where(r8 >= mask_from, d, NEG_BIG)
        return jnp.sum(qq * jnp.exp2(d) * ks, axis=-1, keepdims=True) * vs

    outs = []
    for i in range(c // SUB):
        lo_r, md_r, hi_r = i * SUB, i * SUB + HALF, (i + 1) * SUB
        top, bot = slice(lo_r, md_r), slice(md_r, hi_r)
        ot, ob = o[top], o[bot]
        for s in range(HALF):
            bs, ks, vs = b[lo_r + s:lo_r + s + 1], k[lo_r + s:lo_r + s + 1], v[lo_r + s:lo_r + s + 1]
            ot = ot + pair_term(q[top], b[top], bs, ks, vs, s)
            ob = ob + pair_term(q[bot], b[bot], bs, ks, vs, None)
        for s in range(HALF):
            bs, ks, vs = b[md_r + s:md_r + s + 1], k[md_r + s:md_r + s + 1], v[md_r + s:md_r + s + 1]
            ob = ob + pair_term(q[bot], b[bot], bs, ks, vs, s)
        oi = jnp.concatenate([ot, ob], axis=0)
        if i > 0:
            anchor = b[lo_r - 1:lo_r, :]
            qt = (q[lo_r:hi_r] * jnp.exp2(b[lo_r:hi_r] - anchor)).astype(BF16)
            kt = (k[:lo_r] * jnp.exp2(anchor - b[:lo_r])).astype(BF16)
            sc = _dot_nt(qt, kt)
            oi = oi + _dot(sc.astype(BF16), v[:lo_r].astype(BF16))
        outs.append(oi)
    o = outs[0] if len(outs) == 1 else jnp.concatenate(outs, axis=0)

    kd = (k * jnp.exp2(b_end - b)).astype(BF16)
    st_new = st * jnp.exp2(b_end) + _dot_tn(v.astype(BF16), kd)

    o = o * lax.rsqrt(jnp.mean(o * o, axis=-1, keepdims=True) + EPS)
    return o * ng * _silu(g_raw), st_new


def _hgrn_prompt_kernel(q_ref, f_ref, v_ref, g_ref, lb_ref, ng_ref, o_ref, s_ref, st_ref, *, hb, c, dk):
    t = q_ref.shape[0]
    st_ref[...] = jnp.zeros_like(st_ref)

    def body(ci, carry):
        r0 = pl.multiple_of(ci * c, c)
        for h in range(hb):
            ls = slice(h * dk, (h + 1) * dk)
            o, st_new = _hgrn_chunk(q_ref[pl.ds(r0, c), ls], f_ref[pl.ds(r0, c), ls], v_ref[pl.ds(r0, c), ls],
                                    g_ref[pl.ds(r0, c), ls], lb_ref[:, ls], ng_ref[:, ls], st_ref[h], c)
            st_ref[h] = st_new
            o_ref[pl.ds(r0, c), ls] = o.astype(o_ref.dtype)
        return carry

    lax.fori_loop(0, t // c, body, 0, unroll=4)
    for h in range(hb):
        s_ref[0, h] = st_ref[h].T


def _hgrn_sample_kernel(q_ref, f_ref, v_ref, g_ref, lb_ref, ng_ref, s0_ref, o_ref, s_ref, *, hb, c, dk):
    nb = s0_ref.shape[0]

    def body(bi, carry):
        r0 = pl.multiple_of(bi * c, c)
        for h in range(hb):
            ls = slice(h * dk, (h + 1) * dk)
            o, st_new = _hgrn_chunk(q_ref[pl.ds(r0, c), ls], f_ref[pl.ds(r0, c), ls], v_ref[pl.ds(r0, c), ls],
                                    g_ref[pl.ds(r0, c), ls], lb_ref[:, ls], ng_ref[:, ls], s0_ref[bi, h].T, c)
            s_ref[bi, h] = st_new.T
            o_ref[pl.ds(r0, c), ls] = o.astype(o_ref.dtype)
        return carry

    lax.fori_loop(0, nb, body, 0, unroll=2)


def hgrn_scan(proj, lb, ng, s0, *, row0, nbatch, t, heads, dk, hb):
    w = hb * dk
    nhb = heads // hb
    lb2 = lb.reshape(1, heads * dk)
    ng2 = ng.reshape(1, heads * dk)
    if s0 is None:
        c = min(CHUNK, t)
        rb0 = row0 // t
        kern = functools.partial(_hgrn_prompt_kernel, hb=hb, c=c, dk=dk)
        col = lambda part: pl.BlockSpec((t, w), lambda b, j, part=part: (rb0 + b, part * nhb + j))
        vec = pl.BlockSpec((1, w), lambda b, j: (0, j))
        return pl.pallas_call(
            kern,
            grid=(nbatch, nhb),
            in_specs=[col(0), col(1), col(2), col(3), vec, vec],
            out_specs=[pl.BlockSpec((t, w), lambda b, j: (b, j)),
                       pl.BlockSpec((1, hb, dk, dk), lambda b, j: (b, j, 0, 0))],
            out_shape=[jax.ShapeDtypeStruct((nbatch * t, heads * dk), BF16),
                       jax.ShapeDtypeStruct((nbatch, heads, dk, dk), F32)],
            scratch_shapes=[pltpu.VMEM((hb, dk, dk), F32)],
            compiler_params=_cparams(("parallel", "parallel")),
            name="hgrn_prompt",
        )(proj, proj, proj, proj, lb2, ng2)
    rows = nbatch * t
    rb0 = row0 // rows
    kern = functools.partial(_hgrn_sample_kernel, hb=hb, c=t, dk=dk)
    col = lambda part: pl.BlockSpec((rows, w), lambda j, part=part: (rb0, part * nhb + j))
    vec = pl.BlockSpec((1, w), lambda j: (0, j))
    return pl.pallas_call(
        kern,
        grid=(nhb,),
        in_specs=[col(0), col(1), col(2), col(3), vec, vec,
                  pl.BlockSpec((nbatch, hb, dk, dk), lambda j: (0, j, 0, 0))],
        out_specs=[pl.BlockSpec((rows, w), lambda j: (0, j)),
                   pl.BlockSpec((nbatch, hb, dk, dk), lambda j: (0, j, 0, 0))],
        out_shape=[jax.ShapeDtypeStruct((rows, heads * dk), BF16),
                   jax.ShapeDtypeStruct((nbatch, heads, dk, dk), F32)],
        compiler_params=_cparams(("parallel",)),
        name="hgrn_sample",
    )(proj, proj, proj, proj, lb2, ng2, s0)


def _conv_silu(x, pad_ref, w_ref, bias_ref, c):
    pad_ref[pl.ds(8, c), :] = x
    acc = bias_ref[...] + w_ref[MB_CONV - 1:MB_CONV, :] * x
    for j in range(MB_CONV - 1):
        acc = acc + w_ref[j:j + 1, :] * pad_ref[pl.ds(5 + j, c), :]
    pad_ref[pl.ds(5, 3), :] = pad_ref[pl.ds(5 + c, 3), :]
    return _silu(acc)


def _ssd_kernel(z_ref, x_ref, b_ref, c_ref, dt_ref, *rest, c, heads, has_state):
    if has_state:
        conv0x_ref, conv0b_ref, conv0c_ref, s0_ref = rest[:4]
        rest = rest[4:]
    (wx_ref, wb_ref, wc_ref, bx_ref, bb_ref, bc_ref, dtb_ref, alog_ref, dsk_ref, ng_ref, ep_ref, ec_ref,
     y_ref, s_ref, st_ref, padx_ref, padb_ref, padc_ref) = rest
    p = MB_HEADDIM
    n = MB_DSTATE
    hpg = heads // MB_GROUPS
    gw = hpg * p
    ci = pl.program_id(1)
    nci = pl.num_programs(1)

    @pl.when(ci == 0)
    def _():
        if has_state:
            padx_ref[pl.ds(5, 3), :] = conv0x_ref[0]
            padb_ref[pl.ds(5, 3), :] = conv0b_ref[0]
            padc_ref[pl.ds(5, 3), :] = conv0c_ref[0]
            st_ref[...] = s0_ref[0].T
        else:
            padx_ref[pl.ds(5, 3), :] = jnp.zeros((3, padx_ref.shape[1]), F32)
            padb_ref[pl.ds(5, 3), :] = jnp.zeros((3, padb_ref.shape[1]), F32)
            padc_ref[pl.ds(5, 3), :] = jnp.zeros((3, padc_ref.shape[1]), F32)
            st_ref[...] = jnp.zeros_like(st_ref)

    bm = _conv_silu(b_ref[...], padb_ref, wb_ref, bb_ref, c)
    cm = _conv_silu(c_ref[...], padc_ref, wc_ref, bc_ref, c)

    dtr = dt_ref[...] + dtb_ref[...]
    dt = jnp.maximum(dtr, 0.0) + jnp.log1p(jnp.exp(-jnp.abs(dtr)))
    la = dt * (-jnp.exp(alog_ref[...]))
    row = lax.broadcasted_iota(jnp.int32, (c, c), 0)
    col = lax.broadcasted_iota(jnp.int32, (c, c), 1)
    tri = (row >= col).astype(BF16)
    hi, mid, lo = _split3(la)
    cs = _dot(tri, hi) + _dot(tri, mid) + _dot(tri, lo)
    cs3 = _split3(cs)
    dt3 = _split3(dt)

    mw = hpg * c
    t_m = lax.broadcasted_iota(jnp.int32, (c, mw), 0)
    s_m = lax.broadcasted_iota(jnp.int32, (c, mw), 1) % c
    nbk = LANES // c
    bd_rows = lax.broadcasted_iota(jnp.int32, (LANES, nbk * p), 0) // c
    bd_cols = lax.broadcasted_iota(jnp.int32, (LANES, nbk * p), 1) // p
    bd_mask = bd_rows == bd_cols

    for g in range(MB_GROUPS):
        xl = slice(g * gw, (g + 1) * gw)
        nl = slice(g * n, (g + 1) * n)
        xs = _conv_silu(x_ref[:, xl], padx_ref.at[:, xl], wx_ref.at[:, xl], bx_ref.at[:, xl], c)
        bg = bm[:, nl]
        cg = cm[:, nl]
        ep = ep_ref[:, xl]
        cse = _dot(cs3[0], ep) + _dot(cs3[1], ep) + _dot(cs3[2], ep)
        dte = _dot(dt3[0], ep) + _dot(dt3[1], ep) + _dot(dt3[2], ep)
        cs_end = cse[c - 1:c, :]
        xdt = xs * dte
        xw = xdt * jnp.exp(cs_end - cse)

        st_g = st_ref[:, xl]
        y = _dot(cg.astype(BF16), st_g.astype(BF16)) * jnp.exp(cse)
        st_ref[:, xl] = st_g * jnp.exp(cs_end) + _dot_tn(bg.astype(BF16), xw.astype(BF16))

        ec = ec_ref[:, g * mw:(g + 1) * mw]
        csm = _dot(cs3[0], ec) + _dot(cs3[1], ec) + _dot(cs3[2], ec)
        cs_s = jnp.sum(jnp.where(t_m == s_m, csm, 0.0), axis=0, keepdims=True)
        lw = jnp.exp(jnp.where(t_m >= s_m, csm - cs_s, NEG_BIG))
        b_t = jnp.concatenate([bg] * hpg, axis=0).astype(BF16)
        mm = (_dot_nt(cg.astype(BF16), b_t) * lw).astype(BF16)
        xdt_b = xdt.astype(BF16)
        parts = []
        for blk in range(mw // LANES):
            xb = xdt_b[:, blk * nbk * p:(blk + 1) * nbk * p]
            bd = jnp.where(bd_mask, jnp.concatenate([xb] * nbk, axis=0), jnp.zeros((), BF16))
            parts.append(_dot(mm[:, blk * LANES:(blk + 1) * LANES], bd))
        y = y + jnp.concatenate(parts, axis=1) + dsk_ref[:, xl] * xs

        y = y * _silu(z_ref[:, xl])
        y = y * lax.rsqrt(jnp.mean(y * y, axis=-1, keepdims=True) + EPS)
        y_ref[:, xl] = (y * ng_ref[:, xl]).astype(y_ref.dtype)

    @pl.when(ci == nci - 1)
    def _():
        s_ref[0] = st_ref[...].T


def ssd_scan(zx, dt_raw, conv0, s0, conv_w, conv_b, dt_bias, a_log, d_skip, norm_g, *, row0, nbatch, t, heads):
    p, n = MB_HEADDIM, MB_DSTATE
    di = heads * p
    gn = MB_GROUPS * n
    c = min(CHUNK, t)
    nch = t // c
    rb0 = row0 // c
    has_state = s0 is not None
    head_of_x = jnp.arange(di) // p
    ep = (jnp.arange(heads)[:, None] == head_of_x[None, :]).astype(BF16)
    head_of_m = jnp.arange(heads * c) // c
    ec = (jnp.arange(heads)[:, None] == head_of_m[None, :]).astype(BF16)
    dsk = jnp.repeat(d_skip.astype(F32), p).reshape(1, di)

    row = lambda w, blk: pl.BlockSpec((c, w), lambda b, k, blk=blk: (rb0 + b * nch + k, blk))
    full = lambda a: pl.BlockSpec(a.shape, lambda b, k: (0,) * a.ndim)
    in_specs = [row(di, 0), row(di, 1), row(gn, 2 * di // gn), row(gn, 2 * di // gn + 1), row(heads, 0)]
    args = [zx, zx, zx, zx, dt_raw]
    if has_state:
        in_specs += [pl.BlockSpec((1, MB_CONV - 1, di), lambda b, k: (b, 0, 0)),
                     pl.BlockSpec((1, MB_CONV - 1, gn), lambda b, k: (b, 0, 0)),
                     pl.BlockSpec((1, MB_CONV - 1, gn), lambda b, k: (b, 0, 0)),
                     pl.BlockSpec((1, di, n), lambda b, k: (b, 0, 0))]
        args += [conv0[:, :, :di], conv0[:, :, di:di + gn], conv0[:, :, di + gn:], s0.reshape(nbatch, di, n)]
    consts = [conv_w[:, :di], conv_w[:, di:di + gn], conv_w[:, di + gn:],
              conv_b[:di].reshape(1, di), conv_b[di:di + gn].reshape(1, gn), conv_b[di + gn:].reshape(1, gn),
              dt_bias.reshape(1, heads), a_log.reshape(1, heads), dsk, norm_g.reshape(1, di), ep, ec]
    in_specs += [full(a) for a in consts]
    args += consts
    y, s_fin = pl.pallas_call(
        functools.partial(_ssd_kernel, c=c, heads=heads, has_state=has_state),
        grid=(nbatch, nch),
        in_specs=in_specs,
        out_specs=[pl.BlockSpec((c, di), lambda b, k: (b * nch + k, 0)),
                   pl.BlockSpec((1, di, n), lambda b, k: (b, 0, 0))],
        out_shape=[jax.ShapeDtypeStruct((nbatch * t, di), BF16), jax.ShapeDtypeStruct((nbatch, di, n), F32)],
        scratch_shapes=[pltpu.VMEM((n, di), F32), pltpu.VMEM((c + 8, di), F32),
                        pltpu.VMEM((c + 8, gn), F32), pltpu.VMEM((c + 8, gn), F32)],
        compiler_params=_cparams(("parallel", "arbitrary")),
        name="ssd_state" if has_state else "ssd_prompt",
    )(*args)
    return y, s_fin.reshape(nbatch, heads, p, n)


def _gather_rows_kernel(idx_ref, src_ref, o_ref, buf_ref, sem, *, tg):
    i = pl.program_id(0)

    def copy(r):
        return pltpu.make_async_copy(src_ref.at[pl.ds(idx_ref[i * tg + r], 1)], buf_ref.at[pl.ds(r, 1)], sem)

    def start(r, carry):
        copy(r).start()
        return carry

    def wait(r, carry):
        copy(r).wait()
        return carry

    lax.fori_loop(0, tg, start, 0)
    lax.fori_loop(0, tg, wait, 0)
    o_ref[...] = buf_ref[...].astype(o_ref.dtype)


def gather_rows(src, idx, out_dtype, tg=256):
    rows = idx.shape[0]
    d = src.shape[1]
    return pl.pallas_call(
        functools.partial(_gather_rows_kernel, tg=tg),
        grid_spec=pltpu.PrefetchScalarGridSpec(
            num_scalar_prefetch=1,
            grid=(rows // tg,),
            in_specs=[pl.BlockSpec(memory_space=pl.ANY)],
            out_specs=pl.BlockSpec((tg, d), lambda i, idx: (i, 0)),
            scratch_shapes=[pltpu.VMEM((tg, d), src.dtype), pltpu.SemaphoreType.DMA(())],
        ),
        out_shape=jax.ShapeDtypeStruct((rows, d), out_dtype),
        compiler_params=_cparams(("arbitrary",)),
        name="moe_gather",
    )(idx, src)


def _expert_rows_pipeline(first_ref, ntile_ref, in_hbm, out_hbm, inbuf, outbuf, insem, outsem, *, tmg, tn, n_experts,
                          prepare, compute):
    j = pl.program_id(0)
    e = pl.program_id(1)
    t0 = first_ref[e]
    nt = ntile_ref[e]
    col0 = pl.multiple_of(j * tn, tn)
    is_expert = e < n_experts

    def in_copy(t, slot):
        r0 = pl.multiple_of((t0 + t) * tmg, tmg)
        return pltpu.make_async_copy(in_hbm.at[pl.ds(r0, tmg)], inbuf.at[slot], insem.at[slot])

    def out_copy(t, slot):
        r0 = pl.multiple_of((t0 + t) * tmg, tmg)
        return pltpu.make_async_copy(outbuf.at[slot], out_hbm.at[pl.ds(r0, tmg), pl.ds(col0, tn)], outsem.at[slot])

    @pl.when(jnp.logical_and(is_expert, nt > 0))
    def _():
        in_copy(0, 0).start(priority=1)
        for a in range(1, IN_AHEAD):
            @pl.when(nt > a)
            def _():
                in_copy(a, a).start(priority=1)
        prepare()

    def body(t, carry):
        slot = t % IN_SLOTS
        oslot = t % 2

        @pl.when(t >= 2)
        def _():
            out_copy(t - 2, oslot).wait()

        @pl.when(is_expert)
        def _():
            in_copy(t, slot).wait()

            @pl.when(t + IN_AHEAD < nt)
            def _():
                in_copy(t + IN_AHEAD, (t + IN_AHEAD) % IN_SLOTS).start(priority=1)

            outbuf[oslot] = compute(inbuf[slot]).astype(outbuf.dtype)

        @pl.when(jnp.logical_not(is_expert))
        def _():
            outbuf[oslot] = jnp.zeros(outbuf.shape[1:], outbuf.dtype)

        out_copy(t, oslot).start()
        return carry

    lax.fori_loop(0, nt, body, 0)

    @pl.when(nt >= 2)
    def _():
        out_copy(nt - 2, nt % 2).wait()

    @pl.when(nt >= 1)
    def _():
        out_copy(nt - 1, (nt - 1) % 2).wait()


def _moe_ffn1_kernel(first_ref, ntile_ref, w1_ref, w3_ref, x_hbm, o_hbm, w1s, w3s, xbuf, obuf, xsem, osem, **kw):
    def prepare():
        w1s[...] = w1_ref[0].astype(BF16)
        w3s[...] = w3_ref[0].astype(BF16)

    def compute(a):
        return _silu(_dot(a, w1s[...])) * _dot(a, w3s[...])

    _expert_rows_pipeline(first_ref, ntile_ref, x_hbm, o_hbm, xbuf, obuf, xsem, osem, prepare=prepare,
                          compute=compute, **kw)


def _moe_ffn2_kernel(first_ref, ntile_ref, w2_ref, h_hbm, o_hbm, hbuf, obuf, hsem, osem, **kw):
    _expert_rows_pipeline(first_ref, ntile_ref, h_hbm, o_hbm, hbuf, obuf, hsem, osem, prepare=lambda: None,
                          compute=lambda a: _dot(a, w2_ref[0]), **kw)


def _expert_rows_call(kern, rows, weights, tile_first, tile_count, *, n_out, out_dtype, tmg, tn, extra_scratch, name):
    rp, kdim = rows.shape
    n_experts = weights[0].shape[0]

    def w_map(j, e, first, count):
        return (jnp.minimum(e, n_experts - 1), 0, j)

    return pl.pallas_call(
        functools.partial(kern, tmg=tmg, tn=tn, n_experts=n_experts),
        grid_spec=pltpu.PrefetchScalarGridSpec(
            num_scalar_prefetch=2,
            grid=(n_out // tn, n_experts + 1),
            in_specs=[pl.BlockSpec((1, kdim, tn), w_map) for _ in weights] + [pl.BlockSpec(memory_space=pl.ANY)],
            out_specs=pl.BlockSpec(memory_space=pl.ANY),
            scratch_shapes=extra_scratch + [pltpu.VMEM((IN_SLOTS, tmg, kdim), BF16), pltpu.VMEM((2, tmg, tn), out_dtype),
                                            pltpu.SemaphoreType.DMA((IN_SLOTS,)), pltpu.SemaphoreType.DMA((2,))],
        ),
        out_shape=jax.ShapeDtypeStruct((rp, n_out), out_dtype),
        compiler_params=_cparams(("arbitrary", "arbitrary")),
        name=name,
    )(tile_first, tile_count, *weights, rows)


def moe_ffn1(xs, w1, w3, tile_first, tile_count, *, tmg, tn):
    d, dff = w1.shape[1], w1.shape[2]
    return _expert_rows_call(_moe_ffn1_kernel, xs, [w1, w3], tile_first, tile_count, n_out=dff, out_dtype=BF16,
                             tmg=tmg, tn=tn, extra_scratch=[pltpu.VMEM((d, tn), BF16), pltpu.VMEM((d, tn), BF16)],
                             name="moe_ffn1")


def moe_ffn2(h, w2, tile_first, tile_count, *, tmg, tn):
    return _expert_rows_call(_moe_ffn2_kernel, h, [w2], tile_first, tile_count, n_out=w2.shape[2], out_dtype=F32,
                             tmg=tmg, tn=tn, extra_scratch=[], name="moe_ffn2")


def _combine_norm_kernel(dest_ref, x_ref, gate_ref, y_ref, g_ref, o_ref, buf_ref, sem, *, tc):
    i = pl.program_id(0)

    def copy(r):
        row = dest_ref[i * TOP_K * tc + r]
        return pltpu.make_async_copy(y_ref.at[pl.ds(row, 1)], buf_ref.at[pl.ds(r, 1)], sem)

    def start(r, carry):
        copy(r).start()
        return carry

    def wait(r, carry):
        copy(r).wait()
        return carry

    lax.fori_loop(0, TOP_K * tc, start, 0)
    lax.fori_loop(0, TOP_K * tc, wait, 0)
    x = x_ref[...]
    gate = gate_ref[...]
    for kk in range(TOP_K):
        x = x + gate[:, kk:kk + 1] * buf_ref[pl.ds(kk * tc, tc), :]
    ms = jnp.mean(x * x, axis=-1, keepdims=True)
    o_ref[...] = x * lax.rsqrt(ms + EPS) * g_ref[...]


def combine_norm(x, gate, y_rows, dest, g, tc=128):
    m, d = x.shape
    return pl.pallas_call(
        functools.partial(_combine_norm_kernel, tc=tc),
        grid_spec=pltpu.PrefetchScalarGridSpec(
            num_scalar_prefetch=1,
            grid=(m // tc,),
            in_specs=[pl.BlockSpec((tc, d), lambda i, dest: (i, 0)), pl.BlockSpec((tc, LANES), lambda i, dest: (i, 0)),
                      pl.BlockSpec(memory_space=pl.ANY), pl.BlockSpec((1, d), lambda i, dest: (0, 0))],
            out_specs=pl.BlockSpec((tc, d), lambda i, dest: (i, 0)),
            scratch_shapes=[pltpu.VMEM((TOP_K * tc, d), F32), pltpu.SemaphoreType.DMA(())],
        ),
        out_shape=jax.ShapeDtypeStruct((m, d), F32),
        compiler_params=_cparams(("arbitrary",)),
        name="moe_combine_norm",
    )(dest, x, gate, y_rows, g.reshape(1, d))


def moe_dispatch(top_i, n_experts, tmg):
    m = top_i.shape[0]
    na = m * TOP_K
    nt = na // tmg + n_experts
    flat_e = top_i.reshape(na).astype(jnp.int32)
    order = jnp.argsort(flat_e, stable=True).astype(jnp.int32)
    counts = jnp.zeros((n_experts,), jnp.int32).at[flat_e].add(1)
    padded = (counts + tmg - 1) // tmg * tmg
    pad_end = jnp.cumsum(padded)
    pad_start = pad_end - padded
    cnt_start = jnp.cumsum(counts) - counts
    sorted_e = flat_e[order]
    dest_sorted = pad_start[sorted_e] + jnp.arange(na, dtype=jnp.int32) - cnt_start[sorted_e]
    row_token = jnp.zeros((nt * tmg,), jnp.int32).at[dest_sorted].set(order // TOP_K)
    dest = jnp.zeros((na,), jnp.int32).at[order].set(dest_sorted)
    n_used = (pad_end[-1:] // tmg).astype(jnp.int32)
    tile_first = jnp.concatenate([pad_start // tmg, n_used]).astype(jnp.int32)
    tile_count = jnp.concatenate([padded // tmg, nt - n_used]).astype(jnp.int32)
    return row_token, dest, tile_first, tile_count


def kernel(x_prompt, x_sample, state_hgrn, state_ssm, state_conv, norm_mix, norm_ffn, norm_final, hg_w_in, hg_lb_logits, hg_norm, hg_w_out, mb_w_in, mb_conv_w, mb_conv_b, mb_dt_bias, mb_a_log, mb_d, mb_norm, mb_w_out, ffn_w1, ffn_w3, ffn_w2, moe_router, moe_w1, moe_w3, moe_w2):
    bp, tp, d = x_prompt.shape
    bs, ts, _ = x_sample.shape
    mp, ms = bp * tp, bs * ts
    m = mp + ms
    hg_heads, hg_dk = state_hgrn.shape[2], state_hgrn.shape[3]
    mb_heads = state_ssm.shape[2]
    di = mb_heads * MB_HEADDIM
    conv_dim = state_conv.shape[3]
    n_experts = moe_router.shape[2]
    tm = m // 8
    tm_ws = m // 16

    x = jnp.concatenate([x_prompt.reshape(mp, d), x_sample.reshape(ms, d)], axis=0)
    bf = lambda w: w.astype(BF16)

    lbs = jnp.cumsum(jax.nn.softmax(hg_lb_logits.astype(F32), axis=0), axis=0)
    hn = rmsnorm(x, norm_mix[0], BF16)
    proj = matmul_ws(hn, [hg_w_in[0]], tm=tm_ws, tn=1024, name="hg_in")
    scan = functools.partial(hgrn_scan, proj, lbs[0], hg_norm[0], heads=hg_heads, dk=hg_dk, hb=2)
    o_p, hg_p = scan(None, row0=0, nbatch=bp, t=tp)
    o_s, hg_s = scan(state_hgrn[0], row0=mp, nbatch=bs, t=ts)
    o = jnp.concatenate([o_p, o_s], axis=0)
    x = matmul(o, [bf(hg_w_out[0])], mode="res", res=x, tm=tm, tn=1024, tk=2048, name="hg_out")
    hn = rmsnorm(x, norm_ffn[0], BF16)
    h = matmul_ws(hn, [ffn_w1[0], ffn_w3[0]], mode="swiglu", out_dtype=BF16, tm=tm_ws, tn=512, name="ffn_in")
    x = matmul(h, [bf(ffn_w2[0])], mode="res", res=x, tm=tm, tn=1024, tk=2048, name="ffn_out")

    hn = rmsnorm(x, norm_mix[1], BF16)
    zx = matmul_ws(hn, [mb_w_in[0]], n_out=di + conv_dim, tm=tm_ws, tn=1024, name="mb_in")
    dt_raw = matmul_ws(hn, [mb_w_in[0]], n_out=mb_heads, col_off=(di + conv_dim) // mb_heads, tm=tm,
                       tn=mb_heads, name="mb_in_dt")
    ssd = functools.partial(ssd_scan, zx, dt_raw, conv_w=mb_conv_w[0], conv_b=mb_conv_b[0], dt_bias=mb_dt_bias[0],
                            a_log=mb_a_log[0], d_skip=mb_d[0], norm_g=mb_norm[0], heads=mb_heads)
    y_p, ssm_p = ssd(None, None, row0=0, nbatch=bp, t=tp)
    y_s, ssm_s = ssd(state_conv[0], state_ssm[0], row0=mp, nbatch=bs, t=ts)
    keep = MB_CONV - 1
    last_rows = lambda r0, nb, t: (r0 + jnp.arange(nb)[:, None] * t + (t - keep) + jnp.arange(keep)[None, :]).reshape(-1)
    conv_p = jnp.take(zx, last_rows(0, bp, tp), axis=0)[:, di:].reshape(bp, keep, conv_dim)
    conv_s = jnp.take(zx, last_rows(mp, bs, ts), axis=0)[:, di:].reshape(bs, keep, conv_dim)
    y = jnp.concatenate([y_p, y_s], axis=0)
    x = matmul(y, [bf(mb_w_out[0])], mode="res", res=x, tm=tm, tn=1024, tk=2048, name="mb_out")

    tmg = 256
    hn32, top_i, gates = rmsnorm_router(x, norm_ffn[1], moe_router[0])
    row_token, dest, tile_first, tile_count = moe_dispatch(top_i, n_experts, tmg)
    xs = gather_rows(hn32, row_token, BF16)
    hmid = moe_ffn1(xs, moe_w1[0], moe_w3[0], tile_first, tile_count, tmg=tmg, tn=512)
    yr = moe_ffn2(hmid, bf(moe_w2[0]), tile_first, tile_count, tmg=tmg, tn=512)
    tc = 128
    dest_tiles = dest.reshape(m // tc, tc, TOP_K).transpose(0, 2, 1).reshape(-1)
    yf = combine_norm(x, gates, yr, dest_tiles, norm_final, tc=tc)

    y_prompt = yf[:mp].reshape(bp, tp, d)
    y_sample = yf[mp:].reshape(bs, ts, d)
    return (y_prompt, y_sample, hg_p[None], hg_s[None], ssm_p[None], ssm_s[None], conv_p[None], conv_s[None])
```

```python
import functools

import jax
import jax.numpy as jnp
from jax import lax
from jax.experimental import pallas as pl
from jax.experimental.pallas import tpu as pltpu

EPS = 1e-5
CHUNK = 64
SUB = 16
HALF = SUB // 2
MB_GROUPS = 8
MB_HEADDIM = 64
MB_DSTATE = 128
MB_CONV = 4
TOP_K = 2
NEG_BIG = -1e30
IN_AHEAD = 1
IN_SLOTS = IN_AHEAD + 1

V7X_VMEM_BYTES = 64 * 1024 * 1024
VMEM_LIMIT = V7X_VMEM_BYTES - 4 * 1024 * 1024
LANES = 128

F32 = jnp.float32
BF16 = jnp.bfloat16


def _cparams(sem):
    return pltpu.CompilerParams(dimension_semantics=sem, vmem_limit_bytes=VMEM_LIMIT)


def _sigmoid(x):
    return 1.0 / (1.0 + jnp.exp(-x))


def _silu(x):
    return x * _sigmoid(x)


def _split3(x):
    hi = x.astype(BF16)
    r = x - hi.astype(F32)
    mid = r.astype(BF16)
    lo = (r - mid.astype(F32)).astype(BF16)
    return hi, mid, lo


def _dot(a, b):
    return jnp.dot(a, b, preferred_element_type=F32)


def _dot_nt(a, b):
    return lax.dot_general(a, b, (((1,), (1,)), ((), ())), preferred_element_type=F32)


def _dot_tn(a, b):
    return lax.dot_general(a, b, (((0,), (0,)), ((), ())), preferred_element_type=F32)


def _rmsnorm_kernel(x_ref, g_ref, o_ref):
    x = x_ref[...]
    ms = jnp.mean(x * x, axis=-1, keepdims=True)
    o_ref[...] = (x * lax.rsqrt(ms + EPS) * g_ref[...]).astype(o_ref.dtype)


def rmsnorm(x, g, out_dtype, tr=256):
    m, d = x.shape
    return pl.pallas_call(
        _rmsnorm_kernel,
        grid=(m // tr,),
        in_specs=[pl.BlockSpec((tr, d), lambda i: (i, 0)), pl.BlockSpec((1, d), lambda i: (0, 0))],
        out_specs=pl.BlockSpec((tr, d), lambda i: (i, 0)),
        out_shape=jax.ShapeDtypeStruct((m, d), out_dtype),
        compiler_params=_cparams(("parallel",)),
        name="rmsnorm",
    )(x, g.reshape(1, d))


def _rmsnorm_router_kernel(x_ref, g_ref, wr_ref, o_ref, idx_ref, gate_ref, *, n_experts):
    x = x_ref[...]
    ms = jnp.mean(x * x, axis=-1, keepdims=True)
    hn = x * lax.rsqrt(ms + EPS) * g_ref[...]
    o_ref[...] = hn
    logits = jnp.dot(hn, wr_ref[...], preferred_element_type=F32, precision=lax.Precision.HIGHEST)
    lane = lax.broadcasted_iota(jnp.int32, logits.shape, 1).astype(F32)
    logits = jnp.where(lane < n_experts, logits, -jnp.inf)
    m1 = jnp.max(logits, axis=-1, keepdims=True)
    i1 = jnp.min(jnp.where(logits == m1, lane, float(LANES)), axis=-1, keepdims=True)
    rest = jnp.where(lane == i1, -jnp.inf, logits)
    m2 = jnp.max(rest, axis=-1, keepdims=True)
    i2 = jnp.min(jnp.where(rest == m2, lane, float(LANES)), axis=-1, keepdims=True)
    e2 = jnp.exp(m2 - m1)
    g1 = 1.0 / (1.0 + e2)
    g2 = e2 / (1.0 + e2)
    idx_ref[...] = jnp.where(lane == 0.0, i1, jnp.where(lane == 1.0, i2, 0.0)).astype(jnp.int32)
    gate_ref[...] = jnp.where(lane == 0.0, g1, jnp.where(lane == 1.0, g2, 0.0))


def rmsnorm_router(x, g, w_router, tr=256):
    m, d = x.shape
    n_experts = w_router.shape[1]
    wr = jnp.zeros((d, LANES), F32).at[:, :n_experts].set(w_router)
    hn, idx, gate = pl.pallas_call(
        functools.partial(_rmsnorm_router_kernel, n_experts=n_experts),
        grid=(m // tr,),
        in_specs=[pl.BlockSpec((tr, d), lambda i: (i, 0)), pl.BlockSpec((1, d), lambda i: (0, 0)),
                  pl.BlockSpec((d, LANES), lambda i: (0, 0))],
        out_specs=[pl.BlockSpec((tr, d), lambda i: (i, 0)), pl.BlockSpec((tr, LANES), lambda i: (i, 0)),
                   pl.BlockSpec((tr, LANES), lambda i: (i, 0))],
        out_shape=[jax.ShapeDtypeStruct((m, d), F32), jax.ShapeDtypeStruct((m, LANES), jnp.int32),
                   jax.ShapeDtypeStruct((m, LANES), F32)],
        compiler_params=_cparams(("parallel",)),
        name="rmsnorm_router",
    )(x, g.reshape(1, d), wr)
    return hn, idx[:, :TOP_K], gate


def _epilogue(mode, accs, res_ref, o_ref):
    if mode == "swiglu":
        o_ref[...] = (_silu(accs[0]) * accs[1]).astype(o_ref.dtype)
    elif mode == "res":
        o_ref[...] = res_ref[...] + accs[0]
    else:
        o_ref[...] = accs[0].astype(o_ref.dtype)


def _mm_kernel(*refs, nk, mode):
    n_w = 2 if mode == "swiglu" else 1
    a_ref = refs[0]
    w_refs = refs[1:1 + n_w]
    pos = 1 + n_w
    res_ref = None
    if mode == "res":
        res_ref = refs[pos]
        pos += 1
    o_ref = refs[pos]
    acc_refs = refs[pos + 1:]

    a = a_ref[...]
    parts = [_dot(a, w[...]) for w in w_refs]
    if nk == 1:
        _epilogue(mode, parts, res_ref, o_ref)
        return
    k = pl.program_id(2)

    @pl.when(k == 0)
    def _():
        for acc, p in zip(acc_refs, parts):
            acc[...] = p

    @pl.when(k > 0)
    def _():
        for acc, p in zip(acc_refs, parts):
            acc[...] += p

    @pl.when(k == nk - 1)
    def _():
        _epilogue(mode, [acc[...] for acc in acc_refs], res_ref, o_ref)


def matmul(a, ws, *, mode="plain", res=None, out_dtype=F32, tm, tn, tk, name):
    m, kdim = a.shape
    n_out = ws[0].shape[1]
    nk = kdim // tk
    grid = (m // tm, n_out // tn, nk)
    in_specs = [pl.BlockSpec((tm, tk), lambda i, j, k: (i, k))]
    in_specs += [pl.BlockSpec((tk, tn), lambda i, j, k: (k, j)) for _ in ws]
    args = [a, *ws]
    if mode == "res":
        in_specs.append(pl.BlockSpec((tm, tn), lambda i, j, k: (i, j)))
        args.append(res)
    scratch = [pltpu.VMEM((tm, tn), F32) for _ in ws] if nk > 1 else []
    return pl.pallas_call(
        functools.partial(_mm_kernel, nk=nk, mode=mode),
        grid=grid,
        in_specs=in_specs,
        out_specs=pl.BlockSpec((tm, tn), lambda i, j, k: (i, j)),
        out_shape=jax.ShapeDtypeStruct((m, n_out), out_dtype),
        scratch_shapes=scratch,
        compiler_params=_cparams(("parallel", "parallel", "arbitrary")),
        name=name,
    )(*args)


def _mm_ws_kernel(*refs, mode):
    n_w = 2 if mode == "swiglu" else 1
    a_ref = refs[0]
    w_refs = refs[1:1 + n_w]
    o_ref = refs[1 + n_w]
    ws_refs = refs[2 + n_w:]

    @pl.when(pl.program_id(1) == 0)
    def _():
        for w, ws in zip(w_refs, ws_refs):
            ws[...] = w[...].astype(BF16)

    a = a_ref[...]
    _epilogue(mode, [_dot(a, ws[...]) for ws in ws_refs], None, o_ref)


def matmul_ws(a, ws, *, mode="plain", out_dtype=F32, n_out=None, col_off=0, tm, tn, name):
    m, kdim = a.shape
    n_out = ws[0].shape[1] if n_out is None else n_out
    grid = (n_out // tn, m // tm)
    in_specs = [pl.BlockSpec((tm, kdim), lambda j, i: (i, 0))]
    in_specs += [pl.BlockSpec((kdim, tn), lambda j, i: (0, j + col_off)) for _ in ws]
    return pl.pallas_call(
        functools.partial(_mm_ws_kernel, mode=mode),
        grid=grid,
        in_specs=in_specs,
        out_specs=pl.BlockSpec((tm, tn), lambda j, i: (i, j)),
        out_shape=jax.ShapeDtypeStruct((m, n_out), out_dtype),
        scratch_shapes=[pltpu.VMEM((kdim, tn), BF16) for _ in ws],
        compiler_params=_cparams(("parallel", "arbitrary")),
        name=name,
    )(a, *ws)


def _hgrn_chunk(q_raw, fz, v, g_raw, lb, ng, st, c):
    q = _silu(q_raw)
    f = lb + (1.0 - lb) * _sigmoid(fz)
    lf = jnp.log2(f)
    k = 1.0 - f
    row = lax.broadcasted_iota(jnp.int32, (c, c), 0)
    col = lax.broadcasted_iota(jnp.int32, (c, c), 1)
    tri = (row >= col).astype(BF16)
    hi, mid, lo = _split3(lf)
    b = _dot(tri, hi) + _dot(tri, mid) + _dot(tri, lo)
    b_end = b[c - 1:c, :]

    o = _dot_nt((q * jnp.exp2(b)).astype(BF16), st.astype(BF16))
    r8 = lax.broadcasted_iota(jnp.int32, (HALF, 1), 0)

    def pair_term(qq, bb, bs, ks, vs, mask_from):
        d = bb - bs
        if mask_from is not None:
            d = jnp.where(r8 >= mask_from, d, NEG_BIG)
        return jnp.sum(qq * jnp.exp2(d) * ks, axis=-1, keepdims=True) * vs

    outs = []
    for i in range(c // SUB):
        lo_r, md_r, hi_r = i * SUB, i * SUB + HALF, (i + 1) * SUB
        top, bot = slice(lo_r, md_r), slice(md_r, hi_r)
        ot, ob = o[top], o[bot]
        for s in range(HALF):
            bs, ks, vs = b[lo_r + s:lo_r + s + 1], k[lo_r + s:lo_r + s + 1], v[lo_r + s:lo_r + s + 1]
            ot = ot + pair_term(q[top], b[top], bs, ks, vs, s)
            ob = ob + pair_term(q[bot], b[bot], bs, ks, vs, None)
        for s in range(HALF):
            bs, ks, vs = b[md_r + s:md_r + s + 1], k[md_r + s:md_r + s + 1], v[md_r + s:md_r + s + 1]
            ob = ob + pair_term(q[bot], b[bot], bs, ks, vs, s)
        oi = jnp.concatenate([ot, ob], axis=0)
        if i > 0:
            anchor = b[lo_r - 1:lo_r, :]
            qt = (q[lo_r:hi_r] * jnp.exp2(b[lo_r:hi_r] - anchor)).astype(BF16)
            kt = (k[:lo_r] * jnp.exp2(anchor - b[:lo_r])).astype(BF16)
            sc = _dot_nt(qt, kt)
            oi = oi + _dot(sc.astype(BF16), v[:lo_r].astype(BF16))
        outs.append(oi)
    o = outs[0] if len(outs) == 1 else jnp.concatenate(outs, axis=0)

    kd = (k * jnp.exp2(b_end - b)).astype(BF16)
    st_new = st * jnp.exp2(b_end) + _dot_tn(v.astype(BF16), kd)

    o = o * lax.rsqrt(jnp.mean(o * o, axis=-1, keepdims=True) + EPS)
    return o * ng * _silu(g_raw), st_new


def _hgrn_prompt_kernel(q_ref, f_ref, v_ref, g_ref, lb_ref, ng_ref, o_ref, s_ref, st_ref, *, hb, c, dk):
    t = q_ref.shape[0]
    st_ref[...] = jnp.zeros_like(st_ref)

    def body(ci, carry):
        r0 = pl.multiple_of(ci * c, c)
        for h in range(hb):
            ls = slice(h * dk, (h + 1) * dk)
            o, st_new = _hgrn_chunk(q_ref[pl.ds(r0, c), ls], f_ref[pl.ds(r0, c), ls], v_ref[pl.ds(r0, c), ls],
                                    g_ref[pl.ds(r0, c), ls], lb_ref[:, ls], ng_ref[:, ls], st_ref[h], c)
            st_ref[h] = st_new
            o_ref[pl.ds(r0, c), ls] = o.astype(o_ref.dtype)
        return carry

    lax.fori_loop(0, t // c, body, 0, unroll=8)
    for h in range(hb):
        s_ref[0, h] = st_ref[h].T


def _hgrn_sample_kernel(q_ref, f_ref, v_ref, g_ref, lb_ref, ng_ref, s0_ref, o_ref, s_ref, *, hb, c, dk):
    nb = s0_ref.shape[0]

    def body(bi, carry):
        r0 = pl.multiple_of(bi * c, c)
        for h in range(hb):
            ls = slice(h * dk, (h + 1) * dk)
            o, st_new = _hgrn_chunk(q_ref[pl.ds(r0, c), ls], f_ref[pl.ds(r0, c), ls], v_ref[pl.ds(r0, c), ls],
                                    g_ref[pl.ds(r0, c), ls], lb_ref[:, ls], ng_ref[:, ls], s0_ref[bi, h].T, c)
            s_ref[bi, h] = st_new.T
            o_ref[pl.ds(r0, c), ls] = o.astype(o_ref.dtype)
        return carry

    lax.fori_loop(0, nb, body, 0, unroll=2)


def hgrn_scan(proj, lb, ng, s0, *, row0, nbatch, t, heads, dk, hb):
    w = hb * dk
    nhb = heads // hb
    lb2 = lb.reshape(1, heads * dk)
    ng2 = ng.reshape(1, heads * dk)
    if s0 is None:
        c = min(CHUNK, t)
        rb0 = row0 // t
        kern = functools.partial(_hgrn_prompt_kernel, hb=hb, c=c, dk=dk)
        col = lambda part: pl.BlockSpec((t, w), lambda b, j, part=part: (rb0 + b, part * nhb + j))
        vec = pl.BlockSpec((1, w), lambda b, j: (0, j))
        return pl.pallas_call(
            kern,
            grid=(nbatch, nhb),
            in_specs=[col(0), col(1), col(2), col(3), vec, vec],
            out_specs=[pl.BlockSpec((t, w), lambda b, j: (b, j)),
                       pl.BlockSpec((1, hb, dk, dk), lambda b, j: (b, j, 0, 0))],
            out_shape=[jax.ShapeDtypeStruct((nbatch * t, heads * dk), BF16),
                       jax.ShapeDtypeStruct((nbatch, heads, dk, dk), F32)],
            scratch_shapes=[pltpu.VMEM((hb, dk, dk), F32)],
            compiler_params=_cparams(("parallel", "parallel")),
            name="hgrn_prompt",
        )(proj, proj, proj, proj, lb2, ng2)
    rows = nbatch * t
    rb0 = row0 // rows
    kern = functools.partial(_hgrn_sample_kernel, hb=hb, c=t, dk=dk)
    col = lambda part: pl.BlockSpec((rows, w), lambda j, part=part: (rb0, part * nhb + j))
    vec = pl.BlockSpec((1, w), lambda j: (0, j))
    return pl.pallas_call(
        kern,
        grid=(nhb,),
        in_specs=[col(0), col(1), col(2), col(3), vec, vec,
                  pl.BlockSpec((nbatch, hb, dk, dk), lambda j: (0, j, 0, 0))],
        out_specs=[pl.BlockSpec((rows, w), lambda j: (0, j)),
                   pl.BlockSpec((nbatch, hb, dk, dk), lambda j: (0, j, 0, 0))],
        out_shape=[jax.ShapeDtypeStruct((rows, heads * dk), BF16),
                   jax.ShapeDtypeStruct((nbatch, heads, dk, dk), F32)],
        compiler_params=_cparams(("parallel",)),
        name="hgrn_sample",
    )(proj, proj, proj, proj, lb2, ng2, s0)


def _conv_silu(x, pad_ref, w_ref, bias_ref, c):
    pad_ref[pl.ds(8, c), :] = x
    acc = bias_ref[...] + w_ref[MB_CONV - 1:MB_CONV, :] * x
    for j in range(MB_CONV - 1):
        acc = acc + w_ref[j:j + 1, :] * pad_ref[pl.ds(5 + j, c), :]
    pad_ref[pl.ds(5, 3), :] = pad_ref[pl.ds(5 + c, 3), :]
    return _silu(acc)


def _ssd_kernel(z_ref, x_ref, b_ref, c_ref, dt_ref, *rest, c, heads, has_state):
    if has_state:
        conv0x_ref, conv0b_ref, conv0c_ref, s0_ref = rest[:4]
        rest = rest[4:]
    (wx_ref, wb_ref, wc_ref, bx_ref, bb_ref, bc_ref, dtb_ref, alog_ref, dsk_ref, ng_ref, ep_ref, ec_ref,
     y_ref, s_ref, st_ref, padx_ref, padb_ref, padc_ref) = rest
    p = MB_HEADDIM
    n = MB_DSTATE
    hpg = heads // MB_GROUPS
    gw = hpg * p
    ci = pl.program_id(1)
    nci = pl.num_programs(1)

    @pl.when(ci == 0)
    def _():
        if has_state:
            padx_ref[pl.ds(5, 3), :] = conv0x_ref[0]
            padb_ref[pl.ds(5, 3), :] = conv0b_ref[0]
            padc_ref[pl.ds(5, 3), :] = conv0c_ref[0]
            st_ref[...] = s0_ref[0].T
        else:
            padx_ref[pl.ds(5, 3), :] = jnp.zeros((3, padx_ref.shape[1]), F32)
            padb_ref[pl.ds(5, 3), :] = jnp.zeros((3, padb_ref.shape[1]), F32)
            padc_ref[pl.ds(5, 3), :] = jnp.zeros((3, padc_ref.shape[1]), F32)
            st_ref[...] = jnp.zeros_like(st_ref)

    bm = _conv_silu(b_ref[...], padb_ref, wb_ref, bb_ref, c)
    cm = _conv_silu(c_ref[...], padc_ref, wc_ref, bc_ref, c)

    dtr = dt_ref[...] + dtb_ref[...]
    dt = jnp.maximum(dtr, 0.0) + jnp.log1p(jnp.exp(-jnp.abs(dtr)))
    la = dt * (-jnp.exp(alog_ref[...]))
    row = lax.broadcasted_iota(jnp.int32, (c, c), 0)
    col = lax.broadcasted_iota(jnp.int32, (c, c), 1)
    tri = (row >= col).astype(BF16)
    hi, mid, lo = _split3(la)
    cs = _dot(tri, hi) + _dot(tri, mid) + _dot(tri, lo)
    cs3 = _split3(cs)
    dt3 = _split3(dt)

    mw = hpg * c
    t_m = lax.broadcasted_iota(jnp.int32, (c, mw), 0)
    s_m = lax.broadcasted_iota(jnp.int32, (c, mw), 1) % c
    nbk = LANES // c
    bd_rows = lax.broadcasted_iota(jnp.int32, (LANES, nbk * p), 0) // c
    bd_cols = lax.broadcasted_iota(jnp.int32, (LANES, nbk * p), 1) // p
    bd_mask = bd_rows == bd_cols

    for g in range(MB_GROUPS):
        xl = slice(g * gw, (g + 1) * gw)
        nl = slice(g * n, (g + 1) * n)
        xs = _conv_silu(x_ref[:, xl], padx_ref.at[:, xl], wx_ref.at[:, xl], bx_ref.at[:, xl], c)
        bg = bm[:, nl]
        cg = cm[:, nl]
        ep = ep_ref[:, xl]
        cse = _dot(cs3[0], ep) + _dot(cs3[1], ep) + _dot(cs3[2], ep)
        dte = _dot(dt3[0], ep) + _dot(dt3[1], ep) + _dot(dt3[2], ep)
        cs_end = cse[c - 1:c, :]
        xdt = xs * dte
        xw = xdt * jnp.exp(cs_end - cse)

        st_g = st_ref[:, xl]
        y = _dot(cg.astype(BF16), st_g.astype(BF16)) * jnp.exp(cse)
        st_ref[:, xl] = st_g * jnp.exp(cs_end) + _dot_tn(bg.astype(BF16), xw.astype(BF16))

        ec = ec_ref[:, g * mw:(g + 1) * mw]
        csm = _dot(cs3[0], ec) + _dot(cs3[1], ec) + _dot(cs3[2], ec)
        cs_s = jnp.sum(jnp.where(t_m == s_m, csm, 0.0), axis=0, keepdims=True)
        lw = jnp.exp(jnp.where(t_m >= s_m, csm - cs_s, NEG_BIG))
        b_t = jnp.concatenate([bg] * hpg, axis=0).astype(BF16)
        mm = (_dot_nt(cg.astype(BF16), b_t) * lw).astype(BF16)
        xdt_b = xdt.astype(BF16)
        parts = []
        for blk in range(mw // LANES):
            xb = xdt_b[:, blk * nbk * p:(blk + 1) * nbk * p]
            bd = jnp.where(bd_mask, jnp.concatenate([xb] * nbk, axis=0), jnp.zeros((), BF16))
            parts.append(_dot(mm[:, blk * LANES:(blk + 1) * LANES], bd))
        y = y + jnp.concatenate(parts, axis=1) + dsk_ref[:, xl] * xs

        y = y * _silu(z_ref[:, xl])
        y = y * lax.rsqrt(jnp.mean(y * y, axis=-1, keepdims=True) + EPS)
        y_ref[:, xl] = (y * ng_ref[:, xl]).astype(y_ref.dtype)

    @pl.when(ci == nci - 1)
    def _():
        s_ref[0] = st_ref[...].T


def ssd_scan(zx, dt_raw, conv0, s0, conv_w, conv_b, dt_bias, a_log, d_skip, norm_g, *, row0, nbatch, t, heads):
    p, n = MB_HEADDIM, MB_DSTATE
    di = heads * p
    gn = MB_GROUPS * n
    c = min(CHUNK, t)
    nch = t // c
    rb0 = row0 // c
    has_state = s0 is not None
    head_of_x = jnp.arange(di) // p
    ep = (jnp.arange(heads)[:, None] == head_of_x[None, :]).astype(BF16)
    head_of_m = jnp.arange(heads * c) // c
    ec = (jnp.arange(heads)[:, None] == head_of_m[None, :]).astype(BF16)
    dsk = jnp.repeat(d_skip.astype(F32), p).reshape(1, di)

    row = lambda w, blk: pl.BlockSpec((c, w), lambda b, k, blk=blk: (rb0 + b * nch + k, blk))
    full = lambda a: pl.BlockSpec(a.shape, lambda b, k: (0,) * a.ndim)
    in_specs = [row(di, 0), row(di, 1), row(gn, 2 * di // gn), row(gn, 2 * di // gn + 1), row(heads, 0)]
    args = [zx, zx, zx, zx, dt_raw]
    if has_state:
        in_specs += [pl.BlockSpec((1, MB_CONV - 1, di), lambda b, k: (b, 0, 0)),
                     pl.BlockSpec((1, MB_CONV - 1, gn), lambda b, k: (b, 0, 0)),
                     pl.BlockSpec((1, MB_CONV - 1, gn), lambda b, k: (b, 0, 0)),
                     pl.BlockSpec((1, di, n), lambda b, k: (b, 0, 0))]
        args += [conv0[:, :, :di], conv0[:, :, di:di + gn], conv0[:, :, di + gn:], s0.reshape(nbatch, di, n)]
    consts = [conv_w[:, :di], conv_w[:, di:di + gn], conv_w[:, di + gn:],
              conv_b[:di].reshape(1, di), conv_b[di:di + gn].reshape(1, gn), conv_b[di + gn:].reshape(1, gn),
              dt_bias.reshape(1, heads), a_log.reshape(1, heads), dsk, norm_g.reshape(1, di), ep, ec]
    in_specs += [full(a) for a in consts]
    args += consts
    y, s_fin = pl.pallas_call(
        functools.partial(_ssd_kernel, c=c, heads=heads, has_state=has_state),
        grid=(nbatch, nch),
        in_specs=in_specs,
        out_specs=[pl.BlockSpec((c, di), lambda b, k: (b * nch + k, 0)),
                   pl.BlockSpec((1, di, n), lambda b, k: (b, 0, 0))],
        out_shape=[jax.ShapeDtypeStruct((nbatch * t, di), BF16), jax.ShapeDtypeStruct((nbatch, di, n), F32)],
        scratch_shapes=[pltpu.VMEM((n, di), F32), pltpu.VMEM((c + 8, di), F32),
                        pltpu.VMEM((c + 8, gn), F32), pltpu.VMEM((c + 8, gn), F32)],
        compiler_params=_cparams(("parallel", "arbitrary")),
        name="ssd_state" if has_state else "ssd_prompt",
    )(*args)
    return y, s_fin.reshape(nbatch, heads, p, n)


def _gather_rows_kernel(idx_ref, src_ref, o_ref, buf_ref, sem, *, tg):
    i = pl.program_id(0)

    def copy(r):
        return pltpu.make_async_copy(src_ref.at[pl.ds(idx_ref[i * tg + r], 1)], buf_ref.at[pl.ds(r, 1)], sem)

    def start(r, carry):
        copy(r).start()
        return carry

    def wait(r, carry):
        copy(r).wait()
        return carry

    lax.fori_loop(0, tg, start, 0)
    lax.fori_loop(0, tg, wait, 0)
    o_ref[...] = buf_ref[...].astype(o_ref.dtype)


def gather_rows(src, idx, out_dtype, tg=256):
    rows = idx.shape[0]
    d = src.shape[1]
    return pl.pallas_call(
        functools.partial(_gather_rows_kernel, tg=tg),
        grid_spec=pltpu.PrefetchScalarGridSpec(
            num_scalar_prefetch=1,
            grid=(rows // tg,),
            in_specs=[pl.BlockSpec(memory_space=pl.ANY)],
            out_specs=pl.BlockSpec((tg, d), lambda i, idx: (i, 0)),
            scratch_shapes=[pltpu.VMEM((tg, d), src.dtype), pltpu.SemaphoreType.DMA(())],
        ),
        out_shape=jax.ShapeDtypeStruct((rows, d), out_dtype),
        compiler_params=_cparams(("arbitrary",)),
        name="moe_gather",
    )(idx, src)


def _expert_rows_pipeline(first_ref, ntile_ref, in_hbm, out_hbm, inbuf, outbuf, insem, outsem, *, tmg, tn, n_experts,
                          prepare, compute):
    j = pl.program_id(0)
    e = pl.program_id(1)
    t0 = first_ref[e]
    nt = ntile_ref[e]
    col0 = pl.multiple_of(j * tn, tn)
    is_expert = e < n_experts

    def in_copy(t, slot):
        r0 = pl.multiple_of((t0 + t) * tmg, tmg)
        return pltpu.make_async_copy(in_hbm.at[pl.ds(r0, tmg)], inbuf.at[slot], insem.at[slot])

    def out_copy(t, slot):
        r0 = pl.multiple_of((t0 + t) * tmg, tmg)
        return pltpu.make_async_copy(outbuf.at[slot], out_hbm.at[pl.ds(r0, tmg), pl.ds(col0, tn)], outsem.at[slot])

    @pl.when(jnp.logical_and(is_expert, nt > 0))
    def _():
        in_copy(0, 0).start(priority=1)
        for a in range(1, IN_AHEAD):
            @pl.when(nt > a)
            def _():
                in_copy(a, a).start(priority=1)
        prepare()

    def body(t, carry):
        slot = t % IN_SLOTS
        oslot = t % 2

        @pl.when(t >= 2)
        def _():
            out_copy(t - 2, oslot).wait()

        @pl.when(is_expert)
        def _():
            in_copy(t, slot).wait()

            @pl.when(t + IN_AHEAD < nt)
            def _():
                in_copy(t + IN_AHEAD, (t + IN_AHEAD) % IN_SLOTS).start(priority=1)

            outbuf[oslot] = compute(inbuf[slot]).astype(outbuf.dtype)

        @pl.when(jnp.logical_not(is_expert))
        def _():
            outbuf[oslot] = jnp.zeros(outbuf.shape[1:], outbuf.dtype)

        out_copy(t, oslot).start()
        return carry

    lax.fori_loop(0, nt, body, 0)

    @pl.when(nt >= 2)
    def _():
        out_copy(nt - 2, nt % 2).wait()

    @pl.when(nt >= 1)
    def _():
        out_copy(nt - 1, (nt - 1) % 2).wait()


def _moe_ffn1_kernel(first_ref, ntile_ref, w1_ref, w3_ref, x_hbm, o_hbm, w1s, w3s, xbuf, obuf, xsem, osem, **kw):
    def prepare():
        w1s[...] = w1_ref[0].astype(BF16)
        w3s[...] = w3_ref[0].astype(BF16)

    def compute(a):
        return _silu(_dot(a, w1s[...])) * _dot(a, w3s[...])

    _expert_rows_pipeline(first_ref, ntile_ref, x_hbm, o_hbm, xbuf, obuf, xsem, osem, prepare=prepare,
                          compute=compute, **kw)


def _expert_rows_call(kern, rows, weights, tile_first, tile_count, *, n_out, out_dtype, tmg, tn, extra_scratch, name):
    rp, kdim = rows.shape
    n_experts = weights[0].shape[0]

    def w_map(j, e, first, count):
        return (jnp.minimum(e, n_experts - 1), 0, j)

    return pl.pallas_call(
        functools.partial(kern, tmg=tmg, tn=tn, n_experts=n_experts),
        grid_spec=pltpu.PrefetchScalarGridSpec(
            num_scalar_prefetch=2,
            grid=(n_out // tn, n_experts + 1),
            in_specs=[pl.BlockSpec((1, kdim, tn), w_map) for _ in weights] + [pl.BlockSpec(memory_space=pl.ANY)],
            out_specs=pl.BlockSpec(memory_space=pl.ANY),
            scratch_shapes=extra_scratch + [pltpu.VMEM((IN_SLOTS, tmg, kdim), BF16), pltpu.VMEM((2, tmg, tn), out_dtype),
                                            pltpu.SemaphoreType.DMA((IN_SLOTS,)), pltpu.SemaphoreType.DMA((2,))],
        ),
        out_shape=jax.ShapeDtypeStruct((rp, n_out), out_dtype),
        compiler_params=_cparams(("arbitrary", "arbitrary")),
        name=name,
    )(tile_first, tile_count, *weights, rows)


def moe_ffn1(xs, w1, w3, tile_first, tile_count, *, tmg, tn):
    d, dff = w1.shape[1], w1.shape[2]
    return _expert_rows_call(_moe_ffn1_kernel, xs, [w1, w3], tile_first, tile_count, n_out=dff, out_dtype=BF16,
                             tmg=tmg, tn=tn, extra_scratch=[pltpu.VMEM((d, tn), BF16), pltpu.VMEM((d, tn), BF16)],
                             name="moe_ffn1")


def _moe_ffn2_tiles_kernel(te_ref, nu_ref, h_ref, w2_ref, o_ref):
    used = pl.program_id(1) < nu_ref[0]

    @pl.when(used)
    def _():
        o_ref[...] = _dot(h_ref[...], w2_ref[0])

    @pl.when(jnp.logical_not(used))
    def _():
        o_ref[...] = jnp.zeros_like(o_ref)


def moe_ffn2(h, w2, tile_expert, n_used, *, tmg, tn):
    rp, dff = h.shape
    d = w2.shape[2]

    def a_map(j, i, te, nu):
        return (jnp.minimum(i, nu[0] - 1), 0)

    def w_map(j, i, te, nu):
        return (te[i], 0, j)

    return pl.pallas_call(
        _moe_ffn2_tiles_kernel,
        grid_spec=pltpu.PrefetchScalarGridSpec(
            num_scalar_prefetch=2,
            grid=(d // tn, rp // tmg),
            in_specs=[pl.BlockSpec((tmg, dff), a_map), pl.BlockSpec((1, dff, tn), w_map)],
            out_specs=pl.BlockSpec((tmg, tn), lambda j, i, te, nu: (i, j)),
        ),
        out_shape=jax.ShapeDtypeStruct((rp, d), F32),
        compiler_params=_cparams(("parallel", "arbitrary")),
        name="moe_ffn2",
    )(tile_expert, n_used, h, w2)


def _combine_norm_kernel(dest_ref, x_ref, gate_ref, y_ref, g_ref, o_ref, buf_ref, sem, *, tc):
    i = pl.program_id(0)

    def copy(r):
        row = dest_ref[i * TOP_K * tc + r]
        return pltpu.make_async_copy(y_ref.at[pl.ds(row, 1)], buf_ref.at[pl.ds(r, 1)], sem)

    def start(r, carry):
        copy(r).start()
        return carry

    def wait(r, carry):
        copy(r).wait()
        return carry

    lax.fori_loop(0, TOP_K * tc, start, 0)
    lax.fori_loop(0, TOP_K * tc, wait, 0)
    x = x_ref[...]
    gate = gate_ref[...]
    for kk in range(TOP_K):
        x = x + gate[:, kk:kk + 1] * buf_ref[pl.ds(kk * tc, tc), :]
    ms = jnp.mean(x * x, axis=-1, keepdims=True)
    o_ref[...] = x * lax.rsqrt(ms + EPS) * g_ref[...]


def combine_norm(x, gate, y_rows, dest, g, tc=128):
    m, d = x.shape
    return pl.pallas_call(
        functools.partial(_combine_norm_kernel, tc=tc),
        grid_spec=pltpu.PrefetchScalarGridSpec(
            num_scalar_prefetch=1,
            grid=(m // tc,),
            in_specs=[pl.BlockSpec((tc, d), lambda i, dest: (i, 0)), pl.BlockSpec((tc, LANES), lambda i, dest: (i, 0)),
                      pl.BlockSpec(memory_space=pl.ANY), pl.BlockSpec((1, d), lambda i, dest: (0, 0))],
            out_specs=pl.BlockSpec((tc, d), lambda i, dest: (i, 0)),
            scratch_shapes=[pltpu.VMEM((TOP_K * tc, d), F32), pltpu.SemaphoreType.DMA(())],
        ),
        out_shape=jax.ShapeDtypeStruct((m, d), F32),
        compiler_params=_cparams(("arbitrary",)),
        name="moe_combine_norm",
    )(dest, x, gate, y_rows, g.reshape(1, d))


def moe_dispatch(top_i, n_experts, tmg):
    m = top_i.shape[0]
    na = m * TOP_K
    nt = na // tmg + n_experts
    flat_e = top_i.reshape(na).astype(jnp.int32)
    order = jnp.argsort(flat_e, stable=True).astype(jnp.int32)
    counts = jnp.zeros((n_experts,), jnp.int32).at[flat_e].add(1)
    padded = (counts + tmg - 1) // tmg * tmg
    pad_end = jnp.cumsum(padded)
    pad_start = pad_end - padded
    cnt_start = jnp.cumsum(counts) - counts
    sorted_e = flat_e[order]
    dest_sorted = pad_start[sorted_e] + jnp.arange(na, dtype=jnp.int32) - cnt_start[sorted_e]
    row_token = jnp.zeros((nt * tmg,), jnp.int32).at[dest_sorted].set(order // TOP_K)
    dest = jnp.zeros((na,), jnp.int32).at[order].set(dest_sorted)
    n_used = (pad_end[-1:] // tmg).astype(jnp.int32)
    tile_first = jnp.concatenate([pad_start // tmg, n_used]).astype(jnp.int32)
    tile_count = jnp.concatenate([padded // tmg, nt - n_used]).astype(jnp.int32)
    tile_start = jnp.arange(nt, dtype=jnp.int32) * tmg
    tile_expert = jnp.searchsorted(pad_end, tile_start, side="right").astype(jnp.int32)
    last_expert = tile_expert[jnp.maximum(n_used[0] - 1, 0)]
    tile_expert = jnp.where(tile_start < pad_end[-1], tile_expert, last_expert)
    return row_token, dest, tile_first, tile_count, tile_expert, n_used


def kernel(x_prompt, x_sample, state_hgrn, state_ssm, state_conv, norm_mix, norm_ffn, norm_final, hg_w_in, hg_lb_logits, hg_norm, hg_w_out, mb_w_in, mb_conv_w, mb_conv_b, mb_dt_bias, mb_a_log, mb_d, mb_norm, mb_w_out, ffn_w1, ffn_w3, ffn_w2, moe_router, moe_w1, moe_w3, moe_w2):
    bp, tp, d = x_prompt.shape
    bs, ts, _ = x_sample.shape
    mp, ms = bp * tp, bs * ts
    m = mp + ms
    hg_heads, hg_dk = state_hgrn.shape[2], state_hgrn.shape[3]
    mb_heads = state_ssm.shape[2]
    di = mb_heads * MB_HEADDIM
    conv_dim = state_conv.shape[3]
    n_experts = moe_router.shape[2]
    tm = m // 8
    tm_ws = m // 16

    x = jnp.concatenate([x_prompt.reshape(mp, d), x_sample.reshape(ms, d)], axis=0)
    bf = lambda w: w.astype(BF16)

    lbs = jnp.cumsum(jax.nn.softmax(hg_lb_logits.astype(F32), axis=0), axis=0)
    hn = rmsnorm(x, norm_mix[0], BF16)
    proj = matmul_ws(hn, [hg_w_in[0]], tm=tm_ws, tn=1024, name="hg_in")
    scan = functools.partial(hgrn_scan, proj, lbs[0], hg_norm[0], heads=hg_heads, dk=hg_dk, hb=2)
    o_p, hg_p = scan(None, row0=0, nbatch=bp, t=tp)
    o_s, hg_s = scan(state_hgrn[0], row0=mp, nbatch=bs, t=ts)
    o = jnp.concatenate([o_p, o_s], axis=0)
    x = matmul(o, [bf(hg_w_out[0])], mode="res", res=x, tm=tm, tn=1024, tk=2048, name="hg_out")
    hn = rmsnorm(x, norm_ffn[0], BF16)
    h = matmul_ws(hn, [ffn_w1[0], ffn_w3[0]], mode="swiglu", out_dtype=BF16, tm=tm_ws, tn=512, name="ffn_in")
    x = matmul(h, [bf(ffn_w2[0])], mode="res", res=x, tm=tm, tn=1024, tk=2048, name="ffn_out")

    hn = rmsnorm(x, norm_mix[1], BF16)
    zx = matmul_ws(hn, [mb_w_in[0]], n_out=di + conv_dim, tm=tm_ws, tn=1024, name="mb_in")
    dt_raw = matmul_ws(hn, [mb_w_in[0]], n_out=mb_heads, col_off=(di + conv_dim) // mb_heads, tm=tm,
                       tn=mb_heads, name="mb_in_dt")
    ssd = functools.partial(ssd_scan, zx, dt_raw, conv_w=mb_conv_w[0], conv_b=mb_conv_b[0], dt_bias=mb_dt_bias[0],
                            a_log=mb_a_log[0], d_skip=mb_d[0], norm_g=mb_norm[0], heads=mb_heads)
    y_p, ssm_p = ssd(None, None, row0=0, nbatch=bp, t=tp)
    y_s, ssm_s = ssd(state_conv[0], state_ssm[0], row0=mp, nbatch=bs, t=ts)
    keep = MB_CONV - 1
    last_rows = lambda r0, nb, t: (r0 + jnp.arange(nb)[:, None] * t + (t - keep) + jnp.arange(keep)[None, :]).reshape(-1)
    conv_p = jnp.take(zx, last_rows(0, bp, tp), axis=0)[:, di:].reshape(bp, keep, conv_dim)
    conv_s = jnp.take(zx, last_rows(mp, bs, ts), axis=0)[:, di:].reshape(bs, keep, conv_dim)
    y = jnp.concatenate([y_p, y_s], axis=0)
    x = matmul(y, [bf(mb_w_out[0])], mode="res", res=x, tm=tm, tn=1024, tk=2048, name="mb_out")

    tmg = 256
    hn32, top_i, gates = rmsnorm_router(x, norm_ffn[1], moe_router[0])
    row_token, dest, tile_first, tile_count, tile_expert, n_used = moe_dispatch(top_i, n_experts, tmg)
    xs = gather_rows(hn32, row_token, BF16)
    hmid = moe_ffn1(xs, moe_w1[0], moe_w3[0], tile_first, tile_count, tmg=tmg, tn=512)
    yr = moe_ffn2(hmid, bf(moe_w2[0]), tile_expert, n_used, tmg=tmg, tn=512)
    tc = 128
    dest_tiles = dest.reshape(m // tc, tc, TOP_K).transpose(0, 2, 1).reshape(-1)
    yf = combine_norm(x, gates, yr, dest_tiles, norm_final, tc=tc)

    y_prompt = yf[:mp].reshape(bp, tp, d)
    y_sample = yf[mp:].reshape(bs, ts, d)
    return (y_prompt, y_sample, hg_p[None], hg_s[None], ssm_p[None], ssm_s[None], conv_p[None], conv_s[None])
```
